```python
import math
import jax, jax.numpy as jnp
from jax import lax
import numpy as np

D_MODEL = 1024
BATCH = 4
SEQ = 4096
DEPTH = 1

POOL_WINDOWS = (2, 4, 8, 16)
POOL_GROUPS = len(POOL_WINDOWS)
POOL_GROUP_DIM = D_MODEL // 8
POOL_WIDTH = POOL_GROUPS * POOL_GROUP_DIM
DN_HEAD_DIM = 128
DN_HEADS = D_MODEL // 128
DN_WIDTH = DN_HEADS * DN_HEAD_DIM
CONV_WIDTH = 4
CHUNK = 64
FFN_HIDDEN = ((8 * D_MODEL // 3) + 127) // 128 * 128
N_SUBLAYERS = 3
RMS_EPS = 1e-6
L2_EPS = 1e-6
MIX_IN_SIZES = (POOL_WIDTH, DN_WIDTH, DN_WIDTH, DN_WIDTH, DN_WIDTH, DN_HEADS, DN_HEADS, D_MODEL, D_MODEL)
MIX_IN_WIDTH = int(sum(MIX_IN_SIZES))
MIX_IN_SPLITS = tuple(int(s) for s in np.cumsum(MIX_IN_SIZES)[:-1])

kernel_name = "hybrid_pool_deltanet_macaron_adaln"


def rms_norm(x, g):
    xf = x.astype(jnp.float32)
    xf = xf * lax.rsqrt(jnp.mean(xf * xf, axis=-1, keepdims=True) + RMS_EPS)
    return (xf * g.astype(jnp.float32)).astype(x.dtype)


def l2_normalize(x):
    return x * lax.rsqrt(jnp.sum(x * x, axis=-1, keepdims=True) + L2_EPS)


def modulate(h, shift, scale):
    return h * (1.0 + scale[:, None, :]) + shift[:, None, :]


def swiglu(h, w_in, w_out):
    gate, up = jnp.split(h @ w_in, 2, axis=-1)
    return (jax.nn.silu(gate) * up) @ w_out


def causal_depthwise_conv(x, w):
    C = x.shape[-1]
    return lax.conv_general_dilated(
        x, w[:, None, :].astype(x.dtype), window_strides=(1,),
        padding=[(CONV_WIDTH - 1, 0)],
        dimension_numbers=('NWC', 'WIO', 'NWC'), feature_group_count=C)


def multiscale_pool(xp):
    T = xp.shape[1]
    xf = xp.astype(jnp.float32)
    cs = jnp.cumsum(xf, axis=1)
    pos = jnp.arange(1, T + 1, dtype=jnp.float32)[:, None]
    outs = []
    for gi, w in enumerate(POOL_WINDOWS):
        sl = slice(gi * POOL_GROUP_DIM, (gi + 1) * POOL_GROUP_DIM)
        c_g = cs[..., sl]
        lagged = jnp.pad(c_g, ((0, 0), (w, 0), (0, 0)))[:, :T]
        mean = (c_g - lagged) / jnp.minimum(pos, float(w))
        outs.append(mean - xf[..., sl])
    return jnp.stack(outs, axis=2).astype(xp.dtype)


def gated_delta_rule_chunked(q, k, v, g, beta):
    B, T, H, K = q.shape
    V = v.shape[-1]
    N = T // CHUNK

    def to_chunks(a):
        a = a.reshape((B, N, CHUNK, H) + a.shape[3:])
        return jnp.moveaxis(a, (1, 3), (0, 2))

    qc, kc, vc, bc = to_chunks(q), to_chunks(k), to_chunks(v), to_chunks(beta)
    gc = jnp.cumsum(to_chunks(g), axis=-1)
    causal = jnp.tril(jnp.ones((CHUNK, CHUNK), dtype=bool))
    strict = jnp.tril(jnp.ones((CHUNK, CHUNK), dtype=bool), -1)
    decay = jnp.exp(jnp.where(causal, gc[..., :, None] - gc[..., None, :], -jnp.inf))
    kk = jnp.einsum('nbhik,nbhjk->nbhij', kc, kc)
    a_mat = jnp.where(strict, bc[..., :, None] * kk * decay, 0.0)
    eye = jnp.eye(CHUNK, dtype=q.dtype)
    rhs = jnp.concatenate([bc[..., None] * vc, (bc * jnp.exp(gc))[..., None] * kc], axis=-1)
    sol = lax.linalg.triangular_solve(a_mat + eye, rhs, left_side=True, lower=True,
                                      unit_diagonal=True)
    u, w = sol[..., :V], sol[..., V:]
    qk = jnp.where(causal, jnp.einsum('nbhik,nbhjk->nbhij', qc, kc) * decay, 0.0)
    q_dec = qc * jnp.exp(gc)[..., None]
    g_last = gc[..., -1]
    k_dec = kc * jnp.exp(g_last[..., None] - gc)[..., None]

    def step(S, xs):
        q_i, qk_i, u_i, w_i, k_i, gl_i = xs
        v_new = u_i - jnp.einsum('bhck,bhkv->bhcv', w_i, S)
        o_i = jnp.einsum('bhck,bhkv->bhcv', q_i, S) + jnp.einsum('bhij,bhjv->bhiv', qk_i, v_new)
        S = S * jnp.exp(gl_i)[..., None, None] + jnp.einsum('bhck,bhcv->bhkv', k_i, v_new)
        return S, o_i

    S0 = jnp.zeros((B, H, K, V), dtype=q.dtype)
    _, o = lax.scan(step, S0, (q_dec, qk, u, w, k_dec, g_last))
    return jnp.moveaxis(o, (0, 2), (1, 3)).reshape(B, T, H, V)


def token_mixer(h, mix_w_in, conv_w, a_log, dt_bias, dn_norm_g, pool_w, pool_scale,
                pool_proj, dn_proj, mix_w_out):
    B, T, _ = h.shape
    proj = h @ mix_w_in
    xp, q, k, v, z, b_raw, a_raw, g_pool, g_dn = jnp.split(proj, MIX_IN_SPLITS, axis=-1)

    pooled = multiscale_pool(xp)
    ya = jnp.einsum('btgc,gcd->btgd', pooled, pool_w).reshape(B, T, POOL_WIDTH) * pool_scale
    ya = ya @ pool_proj

    qkv = jax.nn.silu(causal_depthwise_conv(jnp.concatenate([q, k, v], axis=-1), conv_w))
    qkv = qkv.astype(jnp.float32).reshape(B, T, 3, DN_HEADS, DN_HEAD_DIM)
    qh = l2_normalize(qkv[:, :, 0]) * (DN_HEAD_DIM ** -0.5)
    kh = l2_normalize(qkv[:, :, 1])
    vh = qkv[:, :, 2]
    beta = jax.nn.sigmoid(b_raw.astype(jnp.float32))
    g = -jnp.exp(a_log.astype(jnp.float32)) * jax.nn.softplus(
        a_raw.astype(jnp.float32) + dt_bias.astype(jnp.float32))
    o = gated_delta_rule_chunked(qh, kh, vh, g, beta)
    o = rms_norm(o, dn_norm_g).astype(h.dtype).reshape(B, T, DN_WIDTH) * jax.nn.silu(z)
    yb = o @ dn_proj

    merged = jax.nn.sigmoid(g_pool) * ya + jax.nn.sigmoid(g_dn) * yb
    return merged @ mix_w_out


def setup_inputs(seed: int = 0) -> dict:
    key = jax.random.key(seed)
    ks = jax.random.split(key, 20)
    D, L, F = D_MODEL, DEPTH, FFN_HIDDEN

    def nrm(k, shape, fan_in):
        return jax.random.normal(k, shape, jnp.float32) * fan_in ** -0.5

    x = jax.random.normal(ks[0], (BATCH, SEQ, D), jnp.float32)
    c = jax.random.normal(ks[1], (BATCH, D), jnp.float32)
    ada_w = 0.5 * nrm(ks[2], (L, D, N_SUBLAYERS * 3 * D), D)
    ada_b = 0.01 * jax.random.normal(ks[3], (L, N_SUBLAYERS * 3 * D), jnp.float32)
    norm_g = 1.0 + 0.05 * jax.random.normal(ks[4], (L, N_SUBLAYERS, D), jnp.float32)
    ffn1_w_in = nrm(ks[5], (L, D, 2 * F), D)
    ffn1_w_out = nrm(ks[6], (L, F, D), F)
    ffn2_w_in = nrm(ks[7], (L, D, 2 * F), D)
    ffn2_w_out = nrm(ks[8], (L, F, D), F)
    mix_w_in = nrm(ks[9], (L, D, MIX_IN_WIDTH), D)
    conv_w = nrm(ks[10], (L, CONV_WIDTH, 3 * DN_WIDTH), CONV_WIDTH)
    a_log = jnp.log(jax.random.uniform(ks[11], (L, DN_HEADS), jnp.float32, minval=1.0, maxval=16.0))
    dt = jnp.exp(jax.random.uniform(ks[12], (L, DN_HEADS), jnp.float32,
                                    minval=math.log(1e-3), maxval=math.log(1e-1)))
    dt_bias = dt + jnp.log(-jnp.expm1(-dt))
    dn_norm_g = 1.0 + 0.05 * jax.random.normal(ks[13], (L, DN_HEAD_DIM), jnp.float32)
    pool_w = nrm(ks[14], (L, POOL_GROUPS, POOL_GROUP_DIM, POOL_GROUP_DIM), POOL_GROUP_DIM)
    pool_scale = 1.0 + 0.1 * jax.random.normal(ks[15], (L, POOL_WIDTH), jnp.float32)
    pool_proj = nrm(ks[16], (L, POOL_WIDTH, D), POOL_WIDTH)
    dn_proj = nrm(ks[17], (L, DN_WIDTH, D), DN_WIDTH)
    mix_w_out = nrm(ks[18], (L, D, D), D)
    final_g = 1.0 + 0.05 * jax.random.normal(ks[19], (D,), jnp.float32)
    return {"x": x, "c": c, "ada_w": ada_w, "ada_b": ada_b, "norm_g": norm_g,
            "ffn1_w_in": ffn1_w_in, "ffn1_w_out": ffn1_w_out,
            "ffn2_w_in": ffn2_w_in, "ffn2_w_out": ffn2_w_out,
            "mix_w_in": mix_w_in, "conv_w": conv_w, "a_log": a_log, "dt_bias": dt_bias,
            "dn_norm_g": dn_norm_g, "pool_w": pool_w, "pool_scale": pool_scale,
            "pool_proj": pool_proj, "dn_proj": dn_proj, "mix_w_out": mix_w_out,
            "final_g": final_g}


def reference(x, c, ada_w, ada_b, norm_g, ffn1_w_in, ffn1_w_out, ffn2_w_in, ffn2_w_out,
              mix_w_in, conv_w, a_log, dt_bias, dn_norm_g, pool_w, pool_scale, pool_proj,
              dn_proj, mix_w_out, final_g):
    B = x.shape[0]
    for l in range(DEPTH):
        mod = (jax.nn.silu(c) @ ada_w[l] + ada_b[l]).reshape(B, N_SUBLAYERS, 3, D_MODEL)
        shift, scale, gate = mod[:, :, 0], mod[:, :, 1], mod[:, :, 2]

        h = modulate(rms_norm(x, norm_g[l, 0]), shift[:, 0], scale[:, 0])
        x = x + 0.5 * gate[:, 0, None, :] * swiglu(h, ffn1_w_in[l], ffn1_w_out[l])

        h = modulate(rms_norm(x, norm_g[l, 1]), shift[:, 1], scale[:, 1])
        x = x + gate[:, 1, None, :] * token_mixer(
            h, mix_w_in[l], conv_w[l], a_log[l], dt_bias[l], dn_norm_g[l], pool_w[l],
            pool_scale[l], pool_proj[l], dn_proj[l], mix_w_out[l])

        h = modulate(rms_norm(x, norm_g[l, 2]), shift[:, 2], scale[:, 2])
        x = x + 0.5 * gate[:, 2, None, :] * swiglu(h, ffn2_w_in[l], ffn2_w_out[l])
    return rms_norm(x, final_g)
```

```python
import functools

import jax
import jax.numpy as jnp
from jax import lax
from jax.experimental import pallas as pl
from jax.experimental.pallas import tpu as pltpu

F32 = jnp.float32
BF16 = jnp.bfloat16

D_MODEL = 1024
POOL_WINDOWS = (2, 4, 8, 16)
POOL_GROUP_DIM = 128
POOL_WIDTH = 512
DN_HEAD_DIM = 128
DN_HEADS = 8
DN_WIDTH = 1024
CONV_WIDTH = 4
CHUNK = 64
FFN_HIDDEN = 2816
RMS_EPS = 1e-6
L2_EPS = 1e-6

LANES = 128
SUBLANES = 8
VMEM_LIMIT = 56 * 1024 * 1024

TM_FFN = 1024
TH_FFN = 256
TM_MIX = 512
TN_MIX = 512
DN_CHUNKS_PER_STEP = 2
TM_OUT = 512
POOL_HALO = 16
CONV_HALO = 8
MOD_ROWS = 16


def _dot(a, b):
    return jnp.dot(a, b, preferred_element_type=F32)


def _dot_nt(a, b):
    return lax.dot_general(a, b, (((1,), (1,)), ((), ())), preferred_element_type=F32)


def _dot_tn(a, b):
    return lax.dot_general(a, b, (((0,), (0,)), ((), ())), preferred_element_type=F32)


def _sigmoid(x):
    return jax.nn.sigmoid(x)


def _silu(x):
    return x * _sigmoid(x)


def _rms_norm(x, g):
    ms = jnp.mean(x * x, axis=-1, keepdims=True)
    return (x * lax.rsqrt(ms + RMS_EPS)) * g


def _modulated_norm(x, g, mod_ref, sub):
    shift = mod_ref[3 * sub + 0:3 * sub + 1, :]
    scale = mod_ref[3 * sub + 1:3 * sub + 2, :]
    return _rms_norm(x, g) * (1.0 + scale) + shift


def _const_spec(shape):
    nd = len(shape)
    return pl.BlockSpec(shape, lambda *_: (0,) * nd, pipeline_mode=pl.Buffered(1))


def _ada_kernel(c_ref, w_ref, b_ref, o_ref):
    s = _silu(c_ref[...]).astype(BF16)
    o_ref[...] = _dot(s, w_ref[...].astype(BF16)) + b_ref[...]


def _ada(c_pad, ada_w, ada_b):
    n = ada_w.shape[1]
    tn = 1024
    return pl.pallas_call(
        _ada_kernel,
        grid=(n // tn,),
        in_specs=[
            pl.BlockSpec((SUBLANES, D_MODEL), lambda j: (0, 0)),
            pl.BlockSpec((D_MODEL, tn), lambda j: (0, j)),
            pl.BlockSpec((1, tn), lambda j: (0, j)),
        ],
        out_specs=pl.BlockSpec((SUBLANES, tn), lambda j: (0, j)),
        out_shape=jax.ShapeDtypeStruct((SUBLANES, n), F32),
        compiler_params=pltpu.CompilerParams(
            dimension_semantics=("arbitrary",), vmem_limit_bytes=VMEM_LIMIT),
        name="ada",
    )(c_pad, ada_w, ada_b)


def _ffn_kernel(x_ref, mod_ref, g_ref, wg_ref, wu_ref, wo_ref, fg_ref, o_ref,
                h_scr, acc_scr, *, sub, final):
    x = x_ref[...]
    h_scr[...] = _modulated_norm(x, g_ref[...], mod_ref, sub).astype(BF16)
    n_chunks = FFN_HIDDEN // TH_FFN
    for j in range(n_chunks):
        cols = slice(j * TH_FFN, (j + 1) * TH_FFN)
        h = h_scr[...]
        gate = _dot(h, wg_ref[:, cols])
        up = _dot(h, wu_ref[:, cols])
        act = (_silu(gate) * up).astype(BF16)
        part = _dot(act, wo_ref[cols, :])
        if j == 0:
            acc_scr[...] = part
        else:
            acc_scr[...] += part
    res_gate = mod_ref[3 * sub + 2:3 * sub + 3, :]
    y = x_ref[...] + (0.5 * res_gate) * acc_scr[...]
    if final:
        y = _rms_norm(y, fg_ref[...])
    o_ref[...] = y


def _ffn(x2d, mods, norm_g, w_gate, w_up, w_out, final_g, *, sub, final, seq):
    n = x2d.shape[0]
    tiles_per_seq = seq // TM_FFN
    return pl.pallas_call(
        functools.partial(_ffn_kernel, sub=sub, final=final),
        grid=(n // TM_FFN,),
        in_specs=[
            pl.BlockSpec((TM_FFN, D_MODEL), lambda i: (i, 0)),
            pl.BlockSpec((None, MOD_ROWS, D_MODEL), lambda i: (i // tiles_per_seq, 0, 0)),
            _const_spec((1, D_MODEL)),
            _const_spec((D_MODEL, FFN_HIDDEN)),
            _const_spec((D_MODEL, FFN_HIDDEN)),
            _const_spec((FFN_HIDDEN, D_MODEL)),
            _const_spec((1, D_MODEL)),
        ],
        out_specs=pl.BlockSpec((TM_FFN, D_MODEL), lambda i: (i, 0)),
        out_shape=jax.ShapeDtypeStruct((n, D_MODEL), F32),
        scratch_shapes=[
            pltpu.VMEM((TM_FFN, D_MODEL), BF16),
            pltpu.VMEM((TM_FFN, D_MODEL), F32),
        ],
        compiler_params=pltpu.CompilerParams(
            dimension_semantics=("arbitrary",), vmem_limit_bytes=VMEM_LIMIT),
        name="ffn%d" % sub,
    )(x2d, mods, norm_g, w_gate, w_up, w_out, final_g)


def _mix_in_kernel(x_ref, mod_ref, g_ref, w_ref, wba_ref, o_ref, ba_ref, h_scr):
    h_scr[...] = _modulated_norm(x_ref[...], g_ref[...], mod_ref, 1).astype(BF16)
    ba_ref[...] = _dot(h_scr[...], wba_ref[...])
    n_out = o_ref.shape[1]
    for j in range(n_out // TN_MIX):
        cols = slice(j * TN_MIX, (j + 1) * TN_MIX)
        o_ref[:, cols] = _dot(h_scr[...], w_ref[:, cols]).astype(BF16)


def _mix_in(x2d, mods, norm_g, w_big, w_ba, *, seq):
    n = x2d.shape[0]
    n_out = w_big.shape[1]
    tiles_per_seq = seq // TM_MIX
    return pl.pallas_call(
        _mix_in_kernel,
        grid=(n // TM_MIX,),
        in_specs=[
            pl.BlockSpec((TM_MIX, D_MODEL), lambda i: (i, 0)),
            pl.BlockSpec((None, MOD_ROWS, D_MODEL), lambda i: (i // tiles_per_seq, 0, 0)),
            _const_spec((1, D_MODEL)),
            _const_spec((D_MODEL, n_out)),
            _const_spec((D_MODEL, LANES)),
        ],
        out_specs=[
            pl.BlockSpec((TM_MIX, n_out), lambda i: (i, 0)),
            pl.BlockSpec((TM_MIX, LANES), lambda i: (i, 0)),
        ],
        out_shape=[
            jax.ShapeDtypeStruct((n, n_out), BF16),
            jax.ShapeDtypeStruct((n, LANES), F32),
        ],
        scratch_shapes=[pltpu.VMEM((TM_MIX, D_MODEL), BF16)],
        compiler_params=pltpu.CompilerParams(
            dimension_semantics=("arbitrary",), vmem_limit_bytes=VMEM_LIMIT),
        name="mix_in",
    )(x2d, mods, norm_g, w_big, w_ba)


def _unit_lower_inverse(a, eye):
    p = eye - a
    pw = a
    n = 2
    while n < CHUNK:
        pwb = pw.astype(BF16)
        pw = _dot(pwb, pwb)
        p = p + _dot(p.astype(BF16), pw.astype(BF16))
        n *= 2
    return p


def _deltanet_kernel(q_ref, k_ref, v_ref, z_ref, ba_ref, cw_ref, alog_ref, dtb_ref,
                     ng_ref, o_ref, xbuf, s_scr):
    t = pl.program_id(1)
    rows = DN_CHUNKS_PER_STEP * CHUNK

    @pl.when(t == 0)
    def _():
        xbuf[0:CONV_HALO, :] = jnp.zeros((CONV_HALO, 3 * DN_WIDTH), F32)
        s_scr[...] = jnp.zeros(s_scr.shape, F32)

    xbuf[CONV_HALO:CONV_HALO + rows, 0:DN_WIDTH] = q_ref[...].astype(F32)
    xbuf[CONV_HALO:CONV_HALO + rows, DN_WIDTH:2 * DN_WIDTH] = k_ref[...].astype(F32)
    xbuf[CONV_HALO:CONV_HALO + rows, 2 * DN_WIDTH:3 * DN_WIDTH] = v_ref[...].astype(F32)

    ri = lax.broadcasted_iota(jnp.int32, (CHUNK, CHUNK), 0)
    ci = lax.broadcasted_iota(jnp.int32, (CHUNK, CHUNK), 1)
    causal = ri >= ci
    strict = ri > ci
    eye = jnp.where(ri == ci, 1.0, 0.0).astype(F32)
    tri = jnp.where(causal, 1.0, 0.0).astype(BF16)

    neg_decay_rate = -jnp.exp(alog_ref[...])
    dt_bias = dtb_ref[...]
    norm_g = ng_ref[...]

    def conv_silu(r0, col0):
        cols = slice(col0, col0 + DN_HEAD_DIM)
        acc = None
        for j in range(CONV_WIDTH):
            off = CONV_HALO - (CONV_WIDTH - 1) + j
            term = cw_ref[j:j + 1, cols] * xbuf[r0 + off:r0 + off + CHUNK, cols]
            acc = term if acc is None else acc + term
        return _silu(acc)

    def l2n(x):
        return x * lax.rsqrt(jnp.sum(x * x, axis=-1, keepdims=True) + L2_EPS)

    for c in range(DN_CHUNKS_PER_STEP):
        r0 = c * CHUNK
        ba = ba_ref[r0:r0 + CHUNK, :]
        beta = _sigmoid(ba)
        xa = ba + dt_bias
        softplus = jnp.maximum(xa, 0.0) + jnp.log1p(jnp.exp(-jnp.abs(xa)))
        g = neg_decay_rate * softplus
        g_hi = g.astype(BF16)
        r1 = g - g_hi.astype(F32)
        g_mid = r1.astype(BF16)
        g_lo = (r1 - g_mid.astype(F32)).astype(BF16)
        gc = _dot(tri, g_hi) + _dot(tri, g_mid) + _dot(tri, g_lo)
        gc_t = gc.T
        g_last = gc[CHUNK - 1:CHUNK, :]

        for h in range(DN_HEADS):
            cols = slice(h * DN_HEAD_DIM, (h + 1) * DN_HEAD_DIM)
            gl = DN_HEADS + h
            qh = l2n(conv_silu(r0, h * DN_HEAD_DIM)) * (DN_HEAD_DIM ** -0.5)
            kh = l2n(conv_silu(r0, DN_WIDTH + h * DN_HEAD_DIM))
            vh = conv_silu(r0, 2 * DN_WIDTH + h * DN_HEAD_DIM)

            gcol = gc[:, gl:gl + 1]
            grow = gc_t[gl:gl + 1, :]
            bcol = beta[:, h:h + 1]
            glast_h = g_last[:, gl:gl + 1]
            eg = jnp.exp(gcol)
            decay = jnp.exp(jnp.where(causal, gcol - grow, -jnp.inf))

            kb = kh.astype(BF16)
            qkk = _dot_nt(jnp.concatenate([qh.astype(BF16), kb], axis=0), kb)
            qk = qkk[:CHUNK] * decay
            a_mat = jnp.where(strict, bcol * qkk[CHUNK:] * decay, 0.0)
            t_inv = _unit_lower_inverse(a_mat, eye)
            rhs = jnp.concatenate([bcol * vh, (bcol * eg) * kh], axis=1)
            sol = _dot(t_inv.astype(BF16), rhs.astype(BF16))
            u = sol[:, :DN_HEAD_DIM]
            w = sol[:, DN_HEAD_DIM:]
            q_dec = qh * eg
            k_dec = kh * jnp.exp(glast_h - gcol)

            s_h = s_scr[h]
            ws = _dot(jnp.concatenate([w, q_dec], axis=0).astype(BF16), s_h.astype(BF16))
            v_new = u - ws[:CHUNK]
            v_new_b = v_new.astype(BF16)
            o = ws[CHUNK:] + _dot(qk.astype(BF16), v_new_b)
            s_scr[h] = s_h * jnp.exp(glast_h) + _dot_tn(k_dec.astype(BF16), v_new_b)

            z = z_ref[r0:r0 + CHUNK, cols].astype(F32)
            o_ref[r0:r0 + CHUNK, cols] = (_rms_norm(o, norm_g) * _silu(z)).astype(BF16)

    xbuf[0:CONV_HALO, :] = xbuf[rows:rows + CONV_HALO, :]


def _deltanet(big, ba, conv_w, alog_row, dtb_row, dn_norm_g, *, batch, seq):
    rows = DN_CHUNKS_PER_STEP * CHUNK
    steps = seq // rows
    n = batch * seq

    def tok_spec(col_block):
        return pl.BlockSpec((rows, DN_WIDTH), lambda b, t: (b * steps + t, col_block))

    return pl.pallas_call(
        _deltanet_kernel,
        grid=(batch, steps),
        in_specs=[
            tok_spec(0), tok_spec(1), tok_spec(2), tok_spec(3),
            pl.BlockSpec((rows, LANES), lambda b, t: (b * steps + t, 0)),
            pl.BlockSpec((CONV_WIDTH, 3 * DN_WIDTH), lambda b, t: (0, 0)),
            pl.BlockSpec((1, LANES), lambda b, t: (0, 0)),
            pl.BlockSpec((1, LANES), lambda b, t: (0, 0)),
            pl.BlockSpec((1, DN_HEAD_DIM), lambda b, t: (0, 0)),
        ],
        out_specs=pl.BlockSpec((rows, DN_WIDTH), lambda b, t: (b * steps + t, 0)),
        out_shape=jax.ShapeDtypeStruct((n, DN_WIDTH), BF16),
        scratch_shapes=[
            pltpu.VMEM((rows + CONV_HALO, 3 * DN_WIDTH), F32),
            pltpu.VMEM((DN_HEADS, DN_HEAD_DIM, DN_HEAD_DIM), F32),
        ],
        compiler_params=pltpu.CompilerParams(
            dimension_semantics=("arbitrary", "arbitrary"), vmem_limit_bytes=VMEM_LIMIT),
        name="deltanet",
    )(big, big, big, big, ba, conv_w, alog_row, dtb_row, dn_norm_g)


def _mix_out_kernel(x_ref, mod_ref, xp_ref, gp_ref, gd_ref, og_ref, pw_ref, ps_ref,
                    pp_ref, dp_ref, wo_ref, o_ref, pbuf, ya_scr):
    t = pl.program_id(1)

    @pl.when(t == 0)
    def _():
        pbuf[0:POOL_HALO, :] = jnp.zeros((POOL_HALO, POOL_WIDTH), F32)

    pbuf[POOL_HALO:POOL_HALO + TM_OUT, :] = xp_ref[...].astype(F32)

    pos = (t * TM_OUT + 1 + lax.broadcasted_iota(jnp.int32, (TM_OUT, 1), 0)).astype(F32)
    for gi, win in enumerate(POOL_WINDOWS):
        cols = slice(gi * POOL_GROUP_DIM, (gi + 1) * POOL_GROUP_DIM)
        x0 = pbuf[POOL_HALO:POOL_HALO + TM_OUT, cols]
        wsum = x0
        for lag in range(1, win):
            wsum = wsum + pbuf[POOL_HALO - lag:POOL_HALO - lag + TM_OUT, cols]
        pooled = wsum / jnp.minimum(pos, float(win)) - x0
        ya_g = _dot(pooled.astype(BF16), pw_ref[gi]) * ps_ref[:, cols]
        ya_scr[:, cols] = ya_g.astype(BF16)
    pbuf[0:POOL_HALO, :] = pbuf[TM_OUT:TM_OUT + POOL_HALO, :]

    ya = _dot(ya_scr[...], pp_ref[...])
    yb = _dot(og_ref[...], dp_ref[...])
    merged = (_sigmoid(gp_ref[...].astype(F32)) * ya
              + _sigmoid(gd_ref[...].astype(F32)) * yb)
    out = _dot(merged.astype(BF16), wo_ref[...])
    res_gate = mod_ref[5:6, :]
    o_ref[...] = x_ref[...] + res_gate * out


def _mix_out(x2d, mods, big, og, pool_w, pool_scale, pool_proj, dn_proj, w_out, *, batch, seq):
    steps = seq // TM_OUT
    n = batch * seq

    def row_map(b, t):
        return b * steps + t

    return pl.pallas_call(
        _mix_out_kernel,
        grid=(batch, steps),
        in_specs=[
            pl.BlockSpec((TM_OUT, D_MODEL), lambda b, t: (row_map(b, t), 0)),
            pl.BlockSpec((None, MOD_ROWS, D_MODEL), lambda b, t: (b, 0, 0)),
            pl.BlockSpec((TM_OUT, POOL_WIDTH), lambda b, t: (row_map(b, t), 12)),
            pl.BlockSpec((TM_OUT, D_MODEL), lambda b, t: (row_map(b, t), 4)),
            pl.BlockSpec((TM_OUT, D_MODEL), lambda b, t: (row_map(b, t), 5)),
            pl.BlockSpec((TM_OUT, DN_WIDTH), lambda b, t: (row_map(b, t), 0)),
            pl.BlockSpec((len(POOL_WINDOWS), POOL_GROUP_DIM, POOL_GROUP_DIM),
                         lambda b, t: (0, 0, 0)),
            pl.BlockSpec((1, POOL_WIDTH), lambda b, t: (0, 0)),
            pl.BlockSpec((POOL_WIDTH, D_MODEL), lambda b, t: (0, 0)),
            pl.BlockSpec((DN_WIDTH, D_MODEL), lambda b, t: (0, 0)),
            pl.BlockSpec((D_MODEL, D_MODEL), lambda b, t: (0, 0)),
        ],
        out_specs=pl.BlockSpec((TM_OUT, D_MODEL), lambda b, t: (row_map(b, t), 0)),
        out_shape=jax.ShapeDtypeStruct((n, D_MODEL), F32),
        scratch_shapes=[
            pltpu.VMEM((TM_OUT + POOL_HALO, POOL_WIDTH), F32),
            pltpu.VMEM((TM_OUT, POOL_WIDTH), BF16),
        ],
        compiler_params=pltpu.CompilerParams(
            dimension_semantics=("arbitrary", "arbitrary"), vmem_limit_bytes=VMEM_LIMIT),
        name="mix_out",
    )(x2d, mods, big, big, big, og, pool_w, pool_scale, pool_proj, dn_proj, w_out)


def _layer(x2d, c_pad, ada_w, ada_b, norm_g, ffn1_w_in, ffn1_w_out, ffn2_w_in, ffn2_w_out,
           mix_w_in, conv_w, a_log, dt_bias, dn_norm_g, pool_w, pool_scale, pool_proj,
           dn_proj, mix_w_out, final_g, *, batch, seq, final):
    mod = _ada(c_pad, ada_w, ada_b[None, :])
    mods = mod[:batch].reshape(batch, 9, D_MODEL)
    mods = jnp.pad(mods, ((0, 0), (0, MOD_ROWS - 9), (0, 0)))

    def ffn_weights(w_in, w_out):
        w_in = w_in.astype(BF16)
        return w_in[:, :FFN_HIDDEN], w_in[:, FFN_HIDDEN:], w_out.astype(BF16)

    fg = final_g[None, :]
    wg, wu, wo = ffn_weights(ffn1_w_in, ffn1_w_out)
    x2d = _ffn(x2d, mods, norm_g[0][None, :], wg, wu, wo, fg, sub=0, final=False, seq=seq)

    wm = mix_w_in.astype(BF16)
    o_q = POOL_WIDTH
    o_z = o_q + 3 * DN_WIDTH
    o_b = o_z + DN_WIDTH
    o_gp = o_b + 2 * DN_HEADS
    w_big = jnp.concatenate(
        [wm[:, o_q:o_b], wm[:, o_gp:o_gp + 2 * D_MODEL], wm[:, :POOL_WIDTH]], axis=1)
    w_ba = jnp.pad(wm[:, o_b:o_gp], ((0, 0), (0, LANES - 2 * DN_HEADS)))
    big, ba = _mix_in(x2d, mods, norm_g[1][None, :], w_big, w_ba, seq=seq)

    pad_a = (DN_HEADS, LANES - 2 * DN_HEADS)
    alog_row = jnp.pad(a_log, pad_a)[None, :]
    dtb_row = jnp.pad(dt_bias, pad_a)[None, :]
    og = _deltanet(big, ba, conv_w, alog_row, dtb_row, dn_norm_g[None, :],
                   batch=batch, seq=seq)

    x2d = _mix_out(x2d, mods, big, og, pool_w.astype(BF16), pool_scale[None, :],
                   pool_proj.astype(BF16), dn_proj.astype(BF16), mix_w_out.astype(BF16),
                   batch=batch, seq=seq)

    wg, wu, wo = ffn_weights(ffn2_w_in, ffn2_w_out)
    x2d = _ffn(x2d, mods, norm_g[2][None, :], wg, wu, wo, fg, sub=2, final=final, seq=seq)
    return x2d


def kernel(x, c, ada_w, ada_b, norm_g, ffn1_w_in, ffn1_w_out, ffn2_w_in, ffn2_w_out, mix_w_in, conv_w, a_log, dt_bias, dn_norm_g, pool_w, pool_scale, pool_proj, dn_proj, mix_w_out, final_g):
    batch, seq, d = x.shape
    depth = ada_w.shape[0]
    x2d = x.reshape(batch * seq, d)
    c_pad = jnp.pad(c, ((0, SUBLANES - batch), (0, 0)))
    for l in range(depth):
        x2d = _layer(x2d, c_pad, ada_w[l], ada_b[l], norm_g[l], ffn1_w_in[l], ffn1_w_out[l],
                     ffn2_w_in[l], ffn2_w_out[l], mix_w_in[l], conv_w[l], a_log[l],
                     dt_bias[l], dn_norm_g[l], pool_w[l], pool_scale[l], pool_proj[l],
                     dn_proj[l], mix_w_out[l], final_g,
                     batch=batch, seq=seq, final=(l == depth - 1))
    return x2d.reshape(batch, seq, d)
```

```python
import functools

import jax
import jax.numpy as jnp
import numpy as np
from jax import lax
from jax.experimental import pallas as pl
from jax.experimental.pallas import tpu as pltpu

F32 = jnp.float32
BF16 = jnp.bfloat16

D_MODEL = 1024
POOL_WINDOWS = (2, 4, 8, 16)
POOL_GROUP_DIM = 128
POOL_WIDTH = 512
DN_HEAD_DIM = 128
DN_HEADS = 8
DN_WIDTH = 1024
CONV_WIDTH = 4
CHUNK = 64
FFN_HIDDEN = 2816
RMS_EPS = 1e-6
L2_EPS = 1e-6

LANES = 128
SUBLANES = 8
VMEM_LIMIT = 56 * 1024 * 1024

TM_FFN = 1024
TH_FFN = 256
TM_MIX = 512
TN_MIX = 512
DN_CHUNKS_PER_STEP = 4
GROUP_HEADS = 4
N_GROUPS = DN_HEADS // GROUP_HEADS
PACK = GROUP_HEADS * CHUNK
PACKED = DN_HEADS * CHUNK
TM_OUT = 512
POOL_HALO = 16
CONV_HALO = 8
MOD_ROWS = 16


def _dot(a, b):
    return jnp.dot(a, b, preferred_element_type=F32)


def _dot_nt(a, b):
    return lax.dot_general(a, b, (((1,), (1,)), ((), ())), preferred_element_type=F32)


def _dot_tn(a, b):
    return lax.dot_general(a, b, (((0,), (0,)), ((), ())), preferred_element_type=F32)


def _sigmoid(x):
    return jax.nn.sigmoid(x)


def _silu(x):
    return x * _sigmoid(x)


def _rms_norm(x, g):
    ms = jnp.mean(x * x, axis=-1, keepdims=True)
    return (x * lax.rsqrt(ms + RMS_EPS)) * g


def _modulated_norm(x, g, mod_ref, sub):
    shift = mod_ref[3 * sub + 0:3 * sub + 1, :]
    scale = mod_ref[3 * sub + 1:3 * sub + 2, :]
    return _rms_norm(x, g) * (1.0 + scale) + shift


def _const_spec(shape):
    nd = len(shape)
    return pl.BlockSpec(shape, lambda *_: (0,) * nd, pipeline_mode=pl.Buffered(1))


def _ada_kernel(c_ref, w_ref, b_ref, o_ref):
    s = _silu(c_ref[...]).astype(BF16)
    o_ref[...] = _dot(s, w_ref[...].astype(BF16)) + b_ref[...]


def _ada(c_pad, ada_w, ada_b):
    n = ada_w.shape[1]
    tn = 1024
    return pl.pallas_call(
        _ada_kernel,
        grid=(n // tn,),
        in_specs=[
            pl.BlockSpec((SUBLANES, D_MODEL), lambda j: (0, 0)),
            pl.BlockSpec((D_MODEL, tn), lambda j: (0, j)),
            pl.BlockSpec((1, tn), lambda j: (0, j)),
        ],
        out_specs=pl.BlockSpec((SUBLANES, tn), lambda j: (0, j)),
        out_shape=jax.ShapeDtypeStruct((SUBLANES, n), F32),
        compiler_params=pltpu.CompilerParams(
            dimension_semantics=("arbitrary",), vmem_limit_bytes=VMEM_LIMIT),
        name="ada",
    )(c_pad, ada_w, ada_b)


def _ffn_kernel(x_ref, mod_ref, g_ref, wg_ref, wu_ref, wo_ref, fg_ref, o_ref,
                h_scr, acc_scr, *, sub, final):
    x = x_ref[...]
    h_scr[...] = _modulated_norm(x, g_ref[...], mod_ref, sub).astype(BF16)
    n_chunks = FFN_HIDDEN // TH_FFN
    for j in range(n_chunks):
        cols = slice(j * TH_FFN, (j + 1) * TH_FFN)
        h = h_scr[...]
        gate = _dot(h, wg_ref[:, cols])
        up = _dot(h, wu_ref[:, cols])
        act = (_silu(gate) * up).astype(BF16)
        part = _dot(act, wo_ref[cols, :])
        if j == 0:
            acc_scr[...] = part
        else:
            acc_scr[...] += part
    res_gate = mod_ref[3 * sub + 2:3 * sub + 3, :]
    y = x_ref[...] + (0.5 * res_gate) * acc_scr[...]
    if final:
        y = _rms_norm(y, fg_ref[...])
    o_ref[...] = y


def _ffn(x2d, mods, norm_g, w_gate, w_up, w_out, final_g, *, sub, final, seq):
    n = x2d.shape[0]
    tiles_per_seq = seq // TM_FFN
    return pl.pallas_call(
        functools.partial(_ffn_kernel, sub=sub, final=final),
        grid=(n // TM_FFN,),
        in_specs=[
            pl.BlockSpec((TM_FFN, D_MODEL), lambda i: (i, 0)),
            pl.BlockSpec((None, MOD_ROWS, D_MODEL), lambda i: (i // tiles_per_seq, 0, 0)),
            _const_spec((1, D_MODEL)),
            _const_spec((D_MODEL, FFN_HIDDEN)),
            _const_spec((D_MODEL, FFN_HIDDEN)),
            _const_spec((FFN_HIDDEN, D_MODEL)),
            _const_spec((1, D_MODEL)),
        ],
        out_specs=pl.BlockSpec((TM_FFN, D_MODEL), lambda i: (i, 0)),
        out_shape=jax.ShapeDtypeStruct((n, D_MODEL), F32),
        scratch_shapes=[
            pltpu.VMEM((TM_FFN, D_MODEL), BF16),
            pltpu.VMEM((TM_FFN, D_MODEL), F32),
        ],
        compiler_params=pltpu.CompilerParams(
            dimension_semantics=("arbitrary",), vmem_limit_bytes=VMEM_LIMIT),
        name="ffn%d" % sub,
    )(x2d, mods, norm_g, w_gate, w_up, w_out, final_g)


def _mix_in_kernel(x_ref, mod_ref, g_ref, w_ref, wba_ref, o_ref, ba_ref, h_scr):
    h_scr[...] = _modulated_norm(x_ref[...], g_ref[...], mod_ref, 1).astype(BF16)
    ba_ref[...] = _dot(h_scr[...], wba_ref[...])
    n_out = o_ref.shape[1]
    for j in range(n_out // TN_MIX):
        cols = slice(j * TN_MIX, (j + 1) * TN_MIX)
        o_ref[:, cols] = _dot(h_scr[...], w_ref[:, cols]).astype(BF16)


def _mix_in(x2d, mods, norm_g, w_big, w_ba, *, seq):
    n = x2d.shape[0]
    n_out = w_big.shape[1]
    tiles_per_seq = seq // TM_MIX
    return pl.pallas_call(
        _mix_in_kernel,
        grid=(n // TM_MIX,),
        in_specs=[
            pl.BlockSpec((TM_MIX, D_MODEL), lambda i: (i, 0)),
            pl.BlockSpec((None, MOD_ROWS, D_MODEL), lambda i: (i // tiles_per_seq, 0, 0)),
            _const_spec((1, D_MODEL)),
            _const_spec((D_MODEL, n_out)),
            _const_spec((D_MODEL, LANES)),
        ],
        out_specs=[
            pl.BlockSpec((TM_MIX, n_out), lambda i: (i, 0)),
            pl.BlockSpec((TM_MIX, LANES), lambda i: (i, 0)),
        ],
        out_shape=[
            jax.ShapeDtypeStruct((n, n_out), BF16),
            jax.ShapeDtypeStruct((n, LANES), F32),
        ],
        scratch_shapes=[pltpu.VMEM((TM_MIX, D_MODEL), BF16)],
        compiler_params=pltpu.CompilerParams(
            dimension_semantics=("arbitrary",), vmem_limit_bytes=VMEM_LIMIT),
        name="mix_in",
    )(x2d, mods, norm_g, w_big, w_ba)


def _split3(x):
    hi = x.astype(BF16)
    r = x - hi.astype(F32)
    mid = r.astype(BF16)
    lo = (r - mid.astype(F32)).astype(BF16)
    return hi, mid, lo


def _block_diag(blocks):
    n = len(blocks)
    zero = jnp.zeros_like(blocks[0])
    rows = [jnp.concatenate([blocks[i] if j == i else zero for j in range(n)], axis=1)
            for i in range(n)]
    return jnp.concatenate(rows, axis=0)


def _head_cols(h, width=DN_HEAD_DIM):
    return slice(h * width, (h + 1) * width)


def _dn_constants():
    tri = np.tril(np.ones((CHUNK, CHUNK), np.float32))
    ones = np.ones((CHUNK, CHUNK), np.float32)
    tri3 = np.concatenate([tri] * 3, axis=1)
    lg = np.concatenate([tri] * 3 + [-ones] * 3, axis=1)
    e64 = np.zeros((LANES, PACKED), np.float32)
    eb64 = np.zeros((LANES, PACKED), np.float32)
    e128 = np.zeros((LANES, DN_WIDTH), np.float32)
    eb128 = np.zeros((LANES, DN_WIDTH), np.float32)
    for h in range(DN_HEADS):
        e64[DN_HEADS + h, _head_cols(h, CHUNK)] = 1.0
        eb64[h, _head_cols(h, CHUNK)] = 1.0
        e128[DN_HEADS + h, _head_cols(h)] = 1.0
        eb128[h, _head_cols(h)] = 1.0
    eall3 = np.tile(np.concatenate([eb64, eb128, e128], axis=1), (3, 1))
    r = np.arange(CHUNK)[:, None]
    c = np.arange(PACKED)[None, :] % CHUNK
    u3 = np.tile((r <= c).astype(np.float32), (3, 1))
    rr = np.arange(PACK)[:, None] // CHUNK
    cc = np.arange(PACK)[None, :] // CHUNK
    bdm = (rr == cc).astype(np.float32)
    return tuple(jnp.asarray(a, BF16) for a in (tri3, lg, e64, eall3, u3, bdm))


def _deltanet_kernel(q_ref, k_ref, v_ref, z_ref, ba_ref, cw_ref, alog_ref, dtb_ref, ng_ref,
                     tri3_ref, lg_ref, e64_ref, eall3_ref, u3_ref, bdm_ref,
                     o_ref, xbuf, s_scr):
    t = pl.program_id(1)
    rows = DN_CHUNKS_PER_STEP * CHUNK

    @pl.when(t == 0)
    def _():
        xbuf[0:CONV_HALO, :] = jnp.zeros((CONV_HALO, 3 * DN_WIDTH), F32)
        s_scr[...] = jnp.zeros(s_scr.shape, F32)

    xbuf[CONV_HALO:CONV_HALO + rows, 0:DN_WIDTH] = q_ref[...].astype(F32)
    xbuf[CONV_HALO:CONV_HALO + rows, DN_WIDTH:2 * DN_WIDTH] = k_ref[...].astype(F32)
    xbuf[CONV_HALO:CONV_HALO + rows, 2 * DN_WIDTH:3 * DN_WIDTH] = v_ref[...].astype(F32)

    prow = lax.broadcasted_iota(jnp.int32, (CHUNK, PACKED), 0)
    pcol = lax.broadcasted_iota(jnp.int32, (CHUNK, PACKED), 1) & (CHUNK - 1)
    causal_t = prow >= pcol
    strict_t = prow > pcol
    eye_t = jnp.where(prow == pcol, 1.0, 0.0).astype(F32)
    is_beta_lane = lax.broadcasted_iota(jnp.int32, (CHUNK, LANES), 1) < DN_HEADS

    neg_decay_rate = -jnp.exp(alog_ref[...])
    dt_bias = dtb_ref[...]
    norm_g = ng_ref[...]
    tri3 = tri3_ref[...]
    lg = lg_ref[...]
    bdm = bdm_ref[...]

    def conv_silu(r0, sec):
        cols = slice(sec * DN_WIDTH, (sec + 1) * DN_WIDTH)
        acc = None
        for j in range(CONV_WIDTH):
            off = CONV_HALO - (CONV_WIDTH - 1) + j
            term = cw_ref[j:j + 1, cols] * xbuf[r0 + off:r0 + off + CHUNK, cols]
            acc = term if acc is None else acc + term
        return _silu(acc)

    def l2n_heads(x, scale):
        outs = []
        for h in range(DN_HEADS):
            xh = x[:, _head_cols(h)]
            inv = lax.rsqrt(jnp.sum(xh * xh, axis=-1, keepdims=True) + L2_EPS)
            outs.append(xh * (inv * scale))
        return jnp.concatenate(outs, axis=1)

    def group_cols(g, width):
        return slice(g * GROUP_HEADS * width, (g + 1) * GROUP_HEADS * width)

    def head_blocks(x, g):
        return [x[:, _head_cols(g * GROUP_HEADS + i)] for i in range(GROUP_HEADS)]

    def packed_block_diag(wb):
        return jnp.concatenate([wb] * GROUP_HEADS, axis=0) * bdm

    chunks = range(DN_CHUNKS_PER_STEP)

    st = []
    for c in chunks:
        r0 = c * CHUNK
        qn = l2n_heads(conv_silu(r0, 0), DN_HEAD_DIM ** -0.5)
        kn = l2n_heads(conv_silu(r0, 1), 1.0)
        vc = conv_silu(r0, 2)
        ba = ba_ref[r0:r0 + CHUNK, :]
        beta = _sigmoid(ba)
        xa = ba + dt_bias
        softplus = jnp.maximum(xa, 0.0) + jnp.log1p(jnp.exp(-jnp.abs(xa)))
        g = neg_decay_rate * softplus
        st.append(dict(qn=qn, kn=kn, vc=vc, qb=qn.astype(BF16), kb=kn.astype(BF16),
                       beta=beta, gstack=jnp.concatenate(_split3(g), axis=0)))

    for s in st:
        s["gc"] = _dot(tri3, s["gstack"])
        s["geb"] = _dot(s["gstack"], e64_ref[...]).astype(BF16)
    for s in st:
        geb = s.pop("geb")
        s["gdiff"] = _dot(lg, jnp.concatenate([geb, geb * u3_ref[...]], axis=0))
        bgc = jnp.where(is_beta_lane, s.pop("beta"), s.pop("gc"))
        x = _dot(jnp.concatenate(_split3(bgc), axis=1), eall3_ref[...])
        s["beta64"] = x[:, :PACKED]
        s["beta128"] = x[:, PACKED:PACKED + DN_WIDTH]
        s["gc128"] = x[:, PACKED + DN_WIDTH:]
        s.pop("gstack")

    for s in st:
        qk, kk = [], []
        for g in range(N_GROUPS):
            gs = group_cols(g, DN_HEAD_DIM)
            bk = _block_diag(head_blocks(s["kb"], g))
            r = _dot_nt(jnp.concatenate([s["qb"][:, gs], s["kb"][:, gs]], axis=0), bk)
            qk.append(r[:CHUNK])
            kk.append(r[CHUNK:])
        s["qk"] = jnp.concatenate(qk, axis=1)
        s["kk"] = jnp.concatenate(kk, axis=1)
        s.pop("qb")
        s.pop("kb")

    for s in st:
        decay = jnp.exp(jnp.where(causal_t, s.pop("gdiff"), -jnp.inf))
        a_mat = jnp.where(strict_t, s.pop("beta64") * s.pop("kk") * decay, 0.0)
        qkd = s.pop("qk") * decay
        s["qkd"] = [qkd[:, group_cols(g, CHUNK)].astype(BF16) for g in range(N_GROUPS)]
        s["w"] = [-a_mat[:, group_cols(g, CHUNK)] for g in range(N_GROUPS)]
        p0 = eye_t - a_mat
        s["p"] = [p0[:, group_cols(g, CHUNK)] for g in range(N_GROUPS)]

    for s in st:
        for g in range(N_GROUPS):
            wb = s["w"][g].astype(BF16)
            s["w"][g] = _dot(wb, packed_block_diag(wb))
    n = 4
    while n < CHUNK:
        for s in st:
            for g in range(N_GROUPS):
                wb = s["w"][g].astype(BF16)
                r = _dot(jnp.concatenate([wb, s["p"][g].astype(BF16)], axis=0),
                         packed_block_diag(wb))
                s["w"][g] = r[:CHUNK]
                s["p"][g] = s["p"][g] + r[CHUNK:]
        n *= 2
    for s in st:
        for g in range(N_GROUPS):
            s["p"][g] = s["p"][g] + _dot(s["p"][g].astype(BF16),
                                         packed_block_diag(s["w"][g].astype(BF16)))
        s.pop("w")

    for s in st:
        gc128 = s.pop("gc128")
        beta128 = s.pop("beta128")
        eg = jnp.exp(gc128)
        rv = (beta128 * s.pop("vc")).astype(BF16)
        rk = ((beta128 * eg) * s["kn"]).astype(BF16)
        us, ws = [], []
        for g in range(N_GROUPS):
            rhs = jnp.concatenate([_block_diag(head_blocks(rv, g)),
                                   _block_diag(head_blocks(rk, g))], axis=1)
            sol = _dot(s["p"][g].astype(BF16), rhs)
            us.append(sol[:, :GROUP_HEADS * DN_HEAD_DIM])
            ws.append(sol[:, GROUP_HEADS * DN_HEAD_DIM:])
        s.pop("p")
        s["u"] = jnp.concatenate(us, axis=1)
        w_all = jnp.concatenate(ws, axis=1)
        g_last = gc128[CHUNK - 1:CHUNK, :]
        q_dec = s.pop("qn") * eg
        s["wq"] = jnp.concatenate([w_all, q_dec], axis=0).astype(BF16)
        s["kdec"] = (s.pop("kn") * jnp.exp(g_last - gc128)).astype(BF16)
        s["sdecay"] = jnp.exp(g_last)

    state = [s_scr[h] for h in range(DN_HEADS)]
    pair = 2 * DN_HEAD_DIM
    for c in chunks:
        s = st[c]
        r0 = c * CHUNK
        ws = []
        for p in range(DN_HEADS // 2):
            rhs = _block_diag([state[2 * p].astype(BF16), state[2 * p + 1].astype(BF16)])
            ws.append(_dot(s["wq"][:, p * pair:(p + 1) * pair], rhs))
        ws = jnp.concatenate(ws, axis=1)
        v_new = (s["u"] - ws[:CHUNK]).astype(BF16)
        o = []
        for g in range(N_GROUPS):
            o.append(ws[CHUNK:, group_cols(g, DN_HEAD_DIM)]
                     + _dot(s["qkd"][g], _block_diag(head_blocks(v_new, g))))
        o = jnp.concatenate(o, axis=1)
        for p in range(DN_HEADS // 2):
            ps = slice(p * pair, (p + 1) * pair)
            upd = _dot_tn(s["kdec"][:, ps], v_new[:, ps])
            for i in range(2):
                h = 2 * p + i
                blk = slice(i * DN_HEAD_DIM, (i + 1) * DN_HEAD_DIM)
                state[h] = state[h] * s["sdecay"][:, _head_cols(h)] + upd[blk, blk]
        for h in range(DN_HEADS):
            hs = _head_cols(h)
            z = z_ref[r0:r0 + CHUNK, hs].astype(F32)
            o_ref[r0:r0 + CHUNK, hs] = (_rms_norm(o[:, hs], norm_g) * _silu(z)).astype(BF16)

    for h in range(DN_HEADS):
        s_scr[h] = state[h]
    xbuf[0:CONV_HALO, :] = xbuf[rows:rows + CONV_HALO, :]


def _deltanet(big, ba, conv_w, alog_row, dtb_row, dn_norm_g, *, batch, seq):
    rows = DN_CHUNKS_PER_STEP * CHUNK
    steps = seq // rows
    n = batch * seq
    consts = _dn_constants()

    def tok_spec(col_block):
        return pl.BlockSpec((rows, DN_WIDTH), lambda b, t: (b * steps + t, col_block))

    def full_spec(a):
        return pl.BlockSpec(a.shape, lambda b, t: (0,) * a.ndim)

    return pl.pallas_call(
        _deltanet_kernel,
        grid=(batch, steps),
        in_specs=[
            tok_spec(0), tok_spec(1), tok_spec(2), tok_spec(3),
            pl.BlockSpec((rows, LANES), lambda b, t: (b * steps + t, 0)),
            full_spec(conv_w), full_spec(alog_row), full_spec(dtb_row), full_spec(dn_norm_g),
        ] + [full_spec(a) for a in consts],
        out_specs=pl.BlockSpec((rows, DN_WIDTH), lambda b, t: (b * steps + t, 0)),
        out_shape=jax.ShapeDtypeStruct((n, DN_WIDTH), BF16),
        scratch_shapes=[
            pltpu.VMEM((rows + CONV_HALO, 3 * DN_WIDTH), F32),
            pltpu.VMEM((DN_HEADS, DN_HEAD_DIM, DN_HEAD_DIM), F32),
        ],
        compiler_params=pltpu.CompilerParams(
            dimension_semantics=("arbitrary", "arbitrary"), vmem_limit_bytes=VMEM_LIMIT),
        name="deltanet",
    )(big, big, big, big, ba, conv_w, alog_row, dtb_row, dn_norm_g, *consts)


def _mix_out_kernel(x_ref, mod_ref, xp_ref, gp_ref, gd_ref, og_ref, pw_ref, ps_ref,
                    pp_ref, dp_ref, wo_ref, o_ref, pbuf, ya_scr):
    t = pl.program_id(1)

    @pl.when(t == 0)
    def _():
        pbuf[0:POOL_HALO, :] = jnp.zeros((POOL_HALO, POOL_WIDTH), F32)

    pbuf[POOL_HALO:POOL_HALO + TM_OUT, :] = xp_ref[...].astype(F32)

    pos = (t * TM_OUT + 1 + lax.broadcasted_iota(jnp.int32, (TM_OUT, 1), 0)).astype(F32)
    for gi, win in enumerate(POOL_WINDOWS):
        cols = slice(gi * POOL_GROUP_DIM, (gi + 1) * POOL_GROUP_DIM)
        x0 = pbuf[POOL_HALO:POOL_HALO + TM_OUT, cols]
        wsum = x0
        for lag in range(1, win):
            wsum = wsum + pbuf[POOL_HALO - lag:POOL_HALO - lag + TM_OUT, cols]
        pooled = wsum / jnp.minimum(pos, float(win)) - x0
        ya_g = _dot(pooled.astype(BF16), pw_ref[gi]) * ps_ref[:, cols]
        ya_scr[:, cols] = ya_g.astype(BF16)
    pbuf[0:POOL_HALO, :] = pbuf[TM_OUT:TM_OUT + POOL_HALO, :]

    ya = _dot(ya_scr[...], pp_ref[...])
    yb = _dot(og_ref[...], dp_ref[...])
    merged = (_sigmoid(gp_ref[...].astype(F32)) * ya
              + _sigmoid(gd_ref[...].astype(F32)) * yb)
    out = _dot(merged.astype(BF16), wo_ref[...])
    res_gate = mod_ref[5:6, :]
    o_ref[...] = x_ref[...] + res_gate * out


def _mix_out(x2d, mods, big, og, pool_w, pool_scale, pool_proj, dn_proj, w_out, *, batch, seq):
    steps = seq // TM_OUT
    n = batch * seq

    def row_map(b, t):
        return b * steps + t

    return pl.pallas_call(
        _mix_out_kernel,
        grid=(batch, steps),
        in_specs=[
            pl.BlockSpec((TM_OUT, D_MODEL), lambda b, t: (row_map(b, t), 0)),
            pl.BlockSpec((None, MOD_ROWS, D_MODEL), lambda b, t: (b, 0, 0)),
            pl.BlockSpec((TM_OUT, POOL_WIDTH), lambda b, t: (row_map(b, t), 12)),
            pl.BlockSpec((TM_OUT, D_MODEL), lambda b, t: (row_map(b, t), 4)),
            pl.BlockSpec((TM_OUT, D_MODEL), lambda b, t: (row_map(b, t), 5)),
            pl.BlockSpec((TM_OUT, DN_WIDTH), lambda b, t: (row_map(b, t), 0)),
            pl.BlockSpec((len(POOL_WINDOWS), POOL_GROUP_DIM, POOL_GROUP_DIM),
                         lambda b, t: (0, 0, 0)),
            pl.BlockSpec((1, POOL_WIDTH), lambda b, t: (0, 0)),
            pl.BlockSpec((POOL_WIDTH, D_MODEL), lambda b, t: (0, 0)),
            pl.BlockSpec((DN_WIDTH, D_MODEL), lambda b, t: (0, 0)),
            pl.BlockSpec((D_MODEL, D_MODEL), lambda b, t: (0, 0)),
        ],
        out_specs=pl.BlockSpec((TM_OUT, D_MODEL), lambda b, t: (row_map(b, t), 0)),
        out_shape=jax.ShapeDtypeStruct((n, D_MODEL), F32),
        scratch_shapes=[
            pltpu.VMEM((TM_OUT + POOL_HALO, POOL_WIDTH), F32),
            pltpu.VMEM((TM_OUT, POOL_WIDTH), BF16),
        ],
        compiler_params=pltpu.CompilerParams(
            dimension_semantics=("arbitrary", "arbitrary"), vmem_limit_bytes=VMEM_LIMIT),
        name="mix_out",
    )(x2d, mods, big, big, big, og, pool_w, pool_scale, pool_proj, dn_proj, w_out)


def _layer(x2d, c_pad, ada_w, ada_b, norm_g, ffn1_w_in, ffn1_w_out, ffn2_w_in, ffn2_w_out,
           mix_w_in, conv_w, a_log, dt_bias, dn_norm_g, pool_w, pool_scale, pool_proj,
           dn_proj, mix_w_out, final_g, *, batch, seq, final):
    mod = _ada(c_pad, ada_w, ada_b[None, :])
    mods = mod[:batch].reshape(batch, 9, D_MODEL)
    mods = jnp.pad(mods, ((0, 0), (0, MOD_ROWS - 9), (0, 0)))

    def ffn_weights(w_in, w_out):
        w_in = w_in.astype(BF16)
        return w_in[:, :FFN_HIDDEN], w_in[:, FFN_HIDDEN:], w_out.astype(BF16)

    fg = final_g[None, :]
    wg, wu, wo = ffn_weights(ffn1_w_in, ffn1_w_out)
    x2d = _ffn(x2d, mods, norm_g[0][None, :], wg, wu, wo, fg, sub=0, final=False, seq=seq)

    wm = mix_w_in.astype(BF16)
    o_q = POOL_WIDTH
    o_z = o_q + 3 * DN_WIDTH
    o_b = o_z + DN_WIDTH
    o_gp = o_b + 2 * DN_HEADS
    w_big = jnp.concatenate(
        [wm[:, o_q:o_b], wm[:, o_gp:o_gp + 2 * D_MODEL], wm[:, :POOL_WIDTH]], axis=1)
    w_ba = jnp.pad(wm[:, o_b:o_gp], ((0, 0), (0, LANES - 2 * DN_HEADS)))
    big, ba = _mix_in(x2d, mods, norm_g[1][None, :], w_big, w_ba, seq=seq)

    pad_a = (DN_HEADS, LANES - 2 * DN_HEADS)
    alog_row = jnp.pad(a_log, pad_a)[None, :]
    dtb_row = jnp.pad(dt_bias, pad_a)[None, :]
    og = _deltanet(big, ba, conv_w, alog_row, dtb_row, dn_norm_g[None, :],
                   batch=batch, seq=seq)

    x2d = _mix_out(x2d, mods, big, og, pool_w.astype(BF16), pool_scale[None, :],
                   pool_proj.astype(BF16), dn_proj.astype(BF16), mix_w_out.astype(BF16),
                   batch=batch, seq=seq)

    wg, wu, wo = ffn_weights(ffn2_w_in, ffn2_w_out)
    x2d = _ffn(x2d, mods, norm_g[2][None, :], wg, wu, wo, fg, sub=2, final=final, seq=seq)
    return x2d


def kernel(x, c, ada_w, ada_b, norm_g, ffn1_w_in, ffn1_w_out, ffn2_w_in, ffn2_w_out, mix_w_in, conv_w, a_log, dt_bias, dn_norm_g, pool_w, pool_scale, pool_proj, dn_proj, mix_w_out, final_g):
    batch, seq, d = x.shape
    depth = ada_w.shape[0]
    x2d = x.reshape(batch * seq, d)
    c_pad = jnp.pad(c, ((0, SUBLANES - batch), (0, 0)))
    for l in range(depth):
        x2d = _layer(x2d, c_pad, ada_w[l], ada_b[l], norm_g[l], ffn1_w_in[l], ffn1_w_out[l],
                     ffn2_w_in[l], ffn2_w_out[l], mix_w_in[l], conv_w[l], a_log[l],
                     dt_bias[l], dn_norm_g[l], pool_w[l], pool_scale[l], pool_proj[l],
                     dn_proj[l], mix_w_out[l], final_g,
                     batch=batch, seq=seq, final=(l == depth - 1))
    return x2d.reshape(batch, seq, d)
```

```python
import functools

import jax
import jax.numpy as jnp
import numpy as np
from jax import lax
from jax.experimental import pallas as pl
from jax.experimental.pallas import tpu as pltpu

F32 = jnp.float32
BF16 = jnp.bfloat16

D_MODEL = 1024
POOL_WINDOWS = (2, 4, 8, 16)
POOL_GROUP_DIM = 128
POOL_WIDTH = 512
DN_HEAD_DIM = 128
DN_HEADS = 8
DN_WIDTH = 1024
CONV_WIDTH = 4
CHUNK = 64
FFN_HIDDEN = 2816
RMS_EPS = 1e-6
L2_EPS = 1e-6

LANES = 128
SUBLANES = 8
VMEM_LIMIT = 56 * 1024 * 1024

TM_FFN = 1024
TH_FFN = 256
TM_MIX = 512
TN_MIX = 512
EPI_ROWS = 64
DN_CHUNKS_PER_STEP = 8
GROUP_HEADS = 4
N_GROUPS = DN_HEADS // GROUP_HEADS
PACK = GROUP_HEADS * CHUNK
PACKED = DN_HEADS * CHUNK
TM_OUT = 512
POOL_HALO = 16
CONV_HALO = 16
MOD_ROWS = 16
GATE_LANES = 2 * DN_HEADS
GATE_COPIES = 3


def _dot(a, b):
    return jnp.dot(a, b, preferred_element_type=F32)


def _dot_nt(a, b):
    return lax.dot_general(a, b, (((1,), (1,)), ((), ())), preferred_element_type=F32)


def _dot_tn(a, b):
    return lax.dot_general(a, b, (((0,), (0,)), ((), ())), preferred_element_type=F32)


def _sigmoid(x):
    return jax.nn.sigmoid(x)


def _silu(x):
    return x * _sigmoid(x)


def _rms_norm(x, g):
    ms = jnp.mean(x * x, axis=-1, keepdims=True)
    return (x * lax.rsqrt(ms + RMS_EPS)) * g


def _modulated_norm(x, g, mod_ref, sub):
    shift = mod_ref[3 * sub + 0:3 * sub + 1, :]
    scale = mod_ref[3 * sub + 1:3 * sub + 2, :]
    return _rms_norm(x, g) * (1.0 + scale) + shift


def _const_spec(shape):
    nd = len(shape)
    return pl.BlockSpec(shape, lambda *_: (0,) * nd, pipeline_mode=pl.Buffered(1))


def _ada_kernel(c_ref, w_ref, b_ref, o_ref):
    s = _silu(c_ref[...]).astype(BF16)
    o_ref[...] = _dot(s, w_ref[...].astype(BF16)) + b_ref[...]


def _ada(c_pad, ada_w, ada_b):
    n = ada_w.shape[1]
    tn = 1024
    return pl.pallas_call(
        _ada_kernel,
        grid=(n // tn,),
        in_specs=[
            pl.BlockSpec((SUBLANES, D_MODEL), lambda j: (0, 0)),
            pl.BlockSpec((D_MODEL, tn), lambda j: (0, j)),
            pl.BlockSpec((1, tn), lambda j: (0, j)),
        ],
        out_specs=pl.BlockSpec((SUBLANES, tn), lambda j: (0, j)),
        out_shape=jax.ShapeDtypeStruct((SUBLANES, n), F32),
        compiler_params=pltpu.CompilerParams(
            dimension_semantics=("arbitrary",), vmem_limit_bytes=VMEM_LIMIT),
        name="ada",
    )(c_pad, ada_w, ada_b)


def _ffn_kernel(x_ref, mod_ref, g_ref, wi_ref, wo_ref, fg_ref, o_ref,
                h_scr, acc_scr, *, sub, final):
    x = x_ref[...]
    h_scr[...] = _modulated_norm(x, g_ref[...], mod_ref, sub).astype(BF16)
    n_chunks = FFN_HIDDEN // TH_FFN
    for j in range(n_chunks):
        cols = slice(j * TH_FFN, (j + 1) * TH_FFN)
        up_cols = slice(FFN_HIDDEN + j * TH_FFN, FFN_HIDDEN + (j + 1) * TH_FFN)
        h = h_scr[...]
        gate = _dot(h, wi_ref[:, cols])
        up = _dot(h, wi_ref[:, up_cols])
        act = (_silu(gate) * up).astype(BF16)
        part = _dot(act, wo_ref[cols, :])
        if j == 0:
            acc_scr[...] = part
        else:
            acc_scr[...] += part
    res_gate = mod_ref[3 * sub + 2:3 * sub + 3, :]
    y = x_ref[...] + (0.5 * res_gate) * acc_scr[...]
    if final:
        y = _rms_norm(y, fg_ref[...])
    o_ref[...] = y


def _ffn(x2d, mods, norm_g, w_in, w_out, final_g, *, sub, final, seq):
    n = x2d.shape[0]
    tiles_per_seq = seq // TM_FFN
    return pl.pallas_call(
        functools.partial(_ffn_kernel, sub=sub, final=final),
        grid=(n // TM_FFN,),
        in_specs=[
            pl.BlockSpec((TM_FFN, D_MODEL), lambda i: (i, 0)),
            pl.BlockSpec((None, MOD_ROWS, D_MODEL), lambda i: (i // tiles_per_seq, 0, 0)),
            _const_spec((1, D_MODEL)),
            _const_spec((D_MODEL, 2 * FFN_HIDDEN)),
            _const_spec((FFN_HIDDEN, D_MODEL)),
            _const_spec((1, D_MODEL)),
        ],
        out_specs=pl.BlockSpec((TM_FFN, D_MODEL), lambda i: (i, 0)),
        out_shape=jax.ShapeDtypeStruct((n, D_MODEL), F32),
        scratch_shapes=[
            pltpu.VMEM((TM_FFN, D_MODEL), BF16),
            pltpu.VMEM((TM_FFN, D_MODEL), F32),
        ],
        compiler_params=pltpu.CompilerParams(
            dimension_semantics=("arbitrary",), vmem_limit_bytes=VMEM_LIMIT),
        name="ffn%d" % sub,
    )(x2d, mods, norm_g, w_in, w_out, final_g)


def _mix_in_kernel(x_ref, mod_ref, g_ref, w_ref, wba_ref, o_ref, ba_ref, h_scr, phalo):
    t = pl.program_id(1)

    @pl.when(t == 0)
    def _():
        phalo[...] = jnp.zeros(phalo.shape, F32)

    h_scr[...] = _modulated_norm(x_ref[...], g_ref[...], mod_ref, 1).astype(BF16)
    ba_ref[...] = _dot(h_scr[...], wba_ref[...])

    sub_pool = lax.broadcasted_iota(jnp.int32, (SUBLANES, POOL_GROUP_DIM), 0)

    def shift_rows(xb, s, sub):
        n = xb.shape[0] // SUBLANES
        rots = [pltpu.roll(xb[SUBLANES * k:SUBLANES * (k + 1)], s, axis=0) for k in range(n)]
        return jnp.concatenate(
            [jnp.where(sub < s, rots[k - 1], rots[k]) for k in range(1, n)], axis=0)

    def pool_block(acc, j, gi, r0):
        win = POOL_WINDOWS[gi]
        lanes = slice(gi * POOL_GROUP_DIM, (gi + 1) * POOL_GROUP_DIM)
        gl = slice(j * TN_MIX + gi * POOL_GROUP_DIM, j * TN_MIX + (gi + 1) * POOL_GROUP_DIM)
        if r0 == 0:
            xb = jnp.concatenate([phalo[:, lanes], acc[0:EPI_ROWS, lanes]], axis=0)
            phalo[:, lanes] = acc[TM_MIX - POOL_HALO:, lanes]
        else:
            xb = acc[r0 - POOL_HALO:r0 + EPI_ROWS, lanes]
        x0 = xb[POOL_HALO:]
        wsum = xb[SUBLANES:]
        if win > SUBLANES:
            wsum = wsum + xb[:-SUBLANES]
        lag = 1
        while lag < min(win, SUBLANES):
            prev = jnp.concatenate([xb[:SUBLANES], wsum], axis=0)
            wsum = wsum + shift_rows(prev, lag, sub_pool)
            lag *= 2
        wsum = wsum[SUBLANES:]
        pos = (t * TM_MIX + r0 + 1
               + lax.broadcasted_iota(jnp.int32, (EPI_ROWS, 1), 0)).astype(F32)
        pooled = wsum / jnp.minimum(pos, float(win)) - x0
        o_ref[r0:r0 + EPI_ROWS, gl] = pooled.astype(BF16)

    for j in range(o_ref.shape[1] // TN_MIX):
        cols = slice(j * TN_MIX, (j + 1) * TN_MIX)
        acc = _dot(h_scr[...], w_ref[:, cols])
        sec = (j * TN_MIX) // DN_WIDTH
        if sec < 3:
            o_ref[:, cols] = acc.astype(BF16)
        elif sec == 3:
            o_ref[:, cols] = _silu(acc).astype(BF16)
        elif sec < 6:
            o_ref[:, cols] = _sigmoid(acc).astype(BF16)
        else:
            for gi in range(len(POOL_WINDOWS)):
                for r0 in range(0, TM_MIX, EPI_ROWS):
                    pool_block(acc, j, gi, r0)


def _mix_in(x2d, mods, norm_g, w_big, w_ba, *, batch, seq):
    n = x2d.shape[0]
    n_out = w_big.shape[1]
    steps = seq // TM_MIX
    assert TN_MIX == POOL_WIDTH and n_out == 6 * D_MODEL + POOL_WIDTH
    return pl.pallas_call(
        _mix_in_kernel,
        grid=(batch, steps),
        in_specs=[
            pl.BlockSpec((TM_MIX, D_MODEL), lambda b, t: (b * steps + t, 0)),
            pl.BlockSpec((None, MOD_ROWS, D_MODEL), lambda b, t: (b, 0, 0)),
            _const_spec((1, D_MODEL)),
            _const_spec((D_MODEL, n_out)),
            _const_spec((D_MODEL, LANES)),
        ],
        out_specs=[
            pl.BlockSpec((TM_MIX, n_out), lambda b, t: (b * steps + t, 0)),
            pl.BlockSpec((TM_MIX, LANES), lambda b, t: (b * steps + t, 0)),
        ],
        out_shape=[
            jax.ShapeDtypeStruct((n, n_out), BF16),
            jax.ShapeDtypeStruct((n, LANES), F32),
        ],
        scratch_shapes=[
            pltpu.VMEM((TM_MIX, D_MODEL), BF16),
            pltpu.VMEM((POOL_HALO, POOL_WIDTH), F32),
        ],
        compiler_params=pltpu.CompilerParams(
            dimension_semantics=("arbitrary", "arbitrary"), vmem_limit_bytes=VMEM_LIMIT),
        name="mix_in",
    )(x2d, mods, norm_g, w_big, w_ba)


def _split3(x):
    hi = x.astype(BF16)
    r = x - hi.astype(F32)
    mid = r.astype(BF16)
    lo = (r - mid.astype(F32)).astype(BF16)
    return hi, mid, lo


def _block_diag(blocks):
    n = len(blocks)
    zero = jnp.zeros_like(blocks[0])
    rows = [jnp.concatenate([blocks[i] if j == i else zero for j in range(n)], axis=1)
            for i in range(n)]
    return jnp.concatenate(rows, axis=0)


def _head_cols(h, width=DN_HEAD_DIM):
    return slice(h * width, (h + 1) * width)


def _l2n_heads(x, scale):
    outs = []
    for h in range(x.shape[1] // DN_HEAD_DIM):
        xh = x[:, _head_cols(h)]
        inv = lax.rsqrt(jnp.sum(xh * xh, axis=-1, keepdims=True) + L2_EPS)
        outs.append(xh * (inv * scale))
    return jnp.concatenate(outs, axis=1)


def _dn_constants():
    tri = np.tril(np.ones((CHUNK, CHUNK), np.float32))
    tri_blk = np.tile(np.kron(np.eye(DN_CHUNKS_PER_STEP, dtype=np.float32), tri), (1, 3))
    e64 = np.zeros((LANES, PACKED), np.float32)
    eb64 = np.zeros((LANES, PACKED), np.float32)
    e128 = np.zeros((LANES, DN_WIDTH), np.float32)
    eb128 = np.zeros((LANES, DN_WIDTH), np.float32)
    for h in range(DN_HEADS):
        e64[DN_HEADS + h, _head_cols(h, CHUNK)] = 1.0
        eb64[h, _head_cols(h, CHUNK)] = 1.0
        e128[DN_HEADS + h, _head_cols(h)] = 1.0
        eb128[h, _head_cols(h)] = 1.0
    eall = np.concatenate([eb64, e64, eb128, e128], axis=1)
    eall3 = np.zeros_like(eall)
    for p in range(GATE_COPIES):
        eall3[p * GATE_LANES:(p + 1) * GATE_LANES] = eall[:GATE_LANES]
    r = np.arange(CHUNK)[:, None]
    c = np.arange(PACKED)[None, :] % CHUNK
    u3 = np.tile((r <= c).astype(np.float32), (3, 1))
    rr = np.arange(PACK)[:, None] // CHUNK
    cc = np.arange(PACK)[None, :] // CHUNK
    bdm = (rr == cc).astype(np.float32)
    dshift = np.zeros(((CONV_WIDTH - 1) * CHUNK, CONV_HALO + CHUNK), np.float32)
    for s in range(1, CONV_WIDTH):
        dshift[(s - 1) * CHUNK + np.arange(CHUNK), CONV_HALO + np.arange(CHUNK) - s] = 1.0
    return tuple(jnp.asarray(a, BF16) for a in (tri_blk, e64, eall3, u3, bdm, dshift))


def _deltanet_kernel(q_ref, k_ref, v_ref, zg_ref, ba_ref, cw_ref, alog_ref, dtb_ref, ng_ref,
                     trib_ref, e64_ref, eall3_ref, u3_ref, bdm_ref, dshift_ref,
                     o_ref, s_scr, xhalo):
    t = pl.program_id(1)
    rows = DN_CHUNKS_PER_STEP * CHUNK

    @pl.when(t == 0)
    def _():
        s_scr[...] = jnp.zeros(s_scr.shape, F32)
        xhalo[...] = jnp.zeros(xhalo.shape, BF16)

    raw_refs = (q_ref, k_ref, v_ref)

    def conv_silu(c, sec):
        ref = raw_refs[sec]
        cols = slice(sec * DN_WIDTH, (sec + 1) * DN_WIDTH)
        if c == 0:
            xe = jnp.concatenate([xhalo[:, cols], ref[0:CHUNK, :]], axis=0)
        else:
            xe = ref[c * CHUNK - CONV_HALO:(c + 1) * CHUNK, :]
        shifted = _dot(dshift_ref[...], xe)
        y = cw_ref[CONV_WIDTH - 1:CONV_WIDTH, cols] * xe[CONV_HALO:].astype(F32)
        for s in range(1, CONV_WIDTH):
            tap = CONV_WIDTH - 1 - s
            y = y + cw_ref[tap:tap + 1, cols] * shifted[(s - 1) * CHUNK:s * CHUNK]
        return _silu(y)

    prow = lax.broadcasted_iota(jnp.int32, (CHUNK, PACKED), 0)
    pcol = lax.broadcasted_iota(jnp.int32, (CHUNK, PACKED), 1) & (CHUNK - 1)
    causal_t = prow >= pcol
    strict_t = prow > pcol
    eye_t = jnp.where(prow == pcol, 1.0, 0.0).astype(F32)
    gate_lane = lax.broadcasted_iota(jnp.int32, (rows, LANES), 1)
    is_beta_lane = (gate_lane & (GATE_LANES - 1)) < DN_HEADS

    neg_decay_rate = -jnp.exp(alog_ref[...])
    dt_bias = dtb_ref[...]
    norm_g = ng_ref[...]
    bdm = bdm_ref[...]

    def group_cols(g, width):
        return slice(g * GROUP_HEADS * width, (g + 1) * GROUP_HEADS * width)

    def head_blocks(x, g):
        return [x[:, _head_cols(g * GROUP_HEADS + i)] for i in range(GROUP_HEADS)]

    def packed_block_diag(wb):
        return jnp.concatenate([wb] * GROUP_HEADS, axis=0) * bdm

    chunks = range(DN_CHUNKS_PER_STEP)

    ba = ba_ref[...]
    beta = _sigmoid(ba)
    xa = ba + dt_bias
    softplus = jnp.maximum(xa, 0.0) + jnp.log1p(jnp.exp(-jnp.abs(xa)))
    g_log = neg_decay_rate * softplus

    gstack = jnp.concatenate(_split3(g_log), axis=0)
    gc = _dot(trib_ref[...], gstack)
    geb = _dot(gstack, e64_ref[...]).astype(BF16)
    bgc = jnp.where(is_beta_lane, beta, gc)
    hi = bgc.astype(BF16).astype(F32)
    rem = bgc - hi
    mid = rem.astype(BF16).astype(F32)
    piece = jnp.where(gate_lane < GATE_LANES, hi,
                      jnp.where(gate_lane < 2 * GATE_LANES, mid, rem - mid))
    x = _dot(piece.astype(BF16), eall3_ref[...])
    beta64 = x[:, :PACKED]
    gcol64 = x[:, PACKED:2 * PACKED]
    beta128 = x[:, 2 * PACKED:2 * PACKED + DN_WIDTH]
    gc128 = x[:, 2 * PACKED + DN_WIDTH:]
    ones_lhs = jnp.ones((2 * SUBLANES, 3 * CHUNK), BF16)

    st = []
    for c in chunks:
        rs = slice(c * CHUNK, (c + 1) * CHUNK)
        qn = _l2n_heads(conv_silu(c, 0), DN_HEAD_DIM ** -0.5)
        kn = _l2n_heads(conv_silu(c, 1), 1.0)
        pieces = [geb[p * rows + c * CHUNK:p * rows + (c + 1) * CHUNK] for p in range(3)]
        grow = _dot(ones_lhs, jnp.concatenate(pieces, axis=0) * u3_ref[...])[0:1]
        st.append(dict(qb=qn.astype(BF16), kb=kn.astype(BF16), qn=qn, kn=kn,
                       vc=conv_silu(c, 2),
                       gdiff=gcol64[rs] - grow,
                       beta64=beta64[rs], beta128=beta128[rs], gc128=gc128[rs]))

    for s in st:
        qk, kk = [], []
        for g in range(N_GROUPS):
            gs = group_cols(g, DN_HEAD_DIM)
            bk = _block_diag(head_blocks(s["kb"], g))
            r = _dot_nt(jnp.concatenate([s["qb"][:, gs], s["kb"][:, gs]], axis=0), bk)
            qk.append(r[:CHUNK])
            kk.append(r[CHUNK:])
        s["qk"] = jnp.concatenate(qk, axis=1)
        s["kk"] = jnp.concatenate(kk, axis=1)
        s.pop("qb")
        s.pop("kb")

    for s in st:
        decay = jnp.exp(jnp.where(causal_t, s.pop("gdiff"), -jnp.inf))
        a_mat = jnp.where(strict_t, s.pop("beta64") * s.pop("kk") * decay, 0.0)
        qkd = s.pop("qk") * decay
        s["qkd"] = [qkd[:, group_cols(g, CHUNK)].astype(BF16) for g in range(N_GROUPS)]
        s["w"] = [-a_mat[:, group_cols(g, CHUNK)] for g in range(N_GROUPS)]
        p0 = eye_t - a_mat
        s["p"] = [p0[:, group_cols(g, CHUNK)] for g in range(N_GROUPS)]

    for s in st:
        for g in range(N_GROUPS):
            wb = s["w"][g].astype(BF16)
            s["w"][g] = _dot(wb, packed_block_diag(wb))
    n = 4
    while n < CHUNK:
        for s in st:
            for g in range(N_GROUPS):
                wb = s["w"][g].astype(BF16)
                r = _dot(jnp.concatenate([wb, s["p"][g].astype(BF16)], axis=0),
                         packed_block_diag(wb))
                s["w"][g] = r[:CHUNK]
                s["p"][g] = s["p"][g] + r[CHUNK:]
        n *= 2
    for s in st:
        for g in range(N_GROUPS):
            s["p"][g] = s["p"][g] + _dot(s["p"][g].astype(BF16),
                                         packed_block_diag(s["w"][g].astype(BF16)))
        s.pop("w")

    for s in st:
        gc128_c = s.pop("gc128")
        beta128_c = s.pop("beta128")
        eg = jnp.exp(gc128_c)
        rv = (beta128_c * s.pop("vc")).astype(BF16)
        rk = ((beta128_c * eg) * s["kn"]).astype(BF16)
        us, ws = [], []
        for g in range(N_GROUPS):
            rhs = jnp.concatenate([_block_diag(head_blocks(rv, g)),
                                   _block_diag(head_blocks(rk, g))], axis=1)
            sol = _dot(s["p"][g].astype(BF16), rhs)
            us.append(sol[:, :GROUP_HEADS * DN_HEAD_DIM])
            ws.append(sol[:, GROUP_HEADS * DN_HEAD_DIM:])
        s.pop("p")
        s["u"] = jnp.concatenate(us, axis=1)
        w_all = jnp.concatenate(ws, axis=1)
        g_last = gc128_c[CHUNK - 1:CHUNK, :]
        q_dec = s.pop("qn") * eg
        s["wq"] = jnp.concatenate([w_all, q_dec], axis=0).astype(BF16)
        s["kdec"] = (s.pop("kn") * jnp.exp(g_last - gc128_c)).astype(BF16)
        s["sdecay"] = jnp.exp(g_last)

    state = [s_scr[h] for h in range(DN_HEADS)]
    pair = 2 * DN_HEAD_DIM
    for c in chunks:
        s = st[c]
        r0 = c * CHUNK
        ws = []
        for p in range(DN_HEADS // 2):
            rhs = _block_diag([state[2 * p].astype(BF16), state[2 * p + 1].astype(BF16)])
            ws.append(_dot(s["wq"][:, p * pair:(p + 1) * pair], rhs))
        ws = jnp.concatenate(ws, axis=1)
        v_new = (s["u"] - ws[:CHUNK]).astype(BF16)
        o = []
        for g in range(N_GROUPS):
            o.append(ws[CHUNK:, group_cols(g, DN_HEAD_DIM)]
                     + _dot(s["qkd"][g], _block_diag(head_blocks(v_new, g))))
        o = jnp.concatenate(o, axis=1)
        for p in range(DN_HEADS // 2):
            ps = slice(p * pair, (p + 1) * pair)
            upd = _dot_tn(s["kdec"][:, ps], v_new[:, ps])
            for i in range(2):
                h = 2 * p + i
                blk = slice(i * DN_HEAD_DIM, (i + 1) * DN_HEAD_DIM)
                state[h] = state[h] * s["sdecay"][:, _head_cols(h)] + upd[blk, blk]
        for h in range(DN_HEADS):
            hs = _head_cols(h)
            zg = zg_ref[r0:r0 + CHUNK, hs].astype(F32)
            o_ref[r0:r0 + CHUNK, hs] = (_rms_norm(o[:, hs], norm_g) * zg).astype(BF16)

    for h in range(DN_HEADS):
        s_scr[h] = state[h]
    for sec, ref in enumerate(raw_refs):
        xhalo[:, sec * DN_WIDTH:(sec + 1) * DN_WIDTH] = ref[rows - CONV_HALO:rows, :]


def _deltanet(big, ba, conv_w, alog_row, dtb_row, dn_norm_g, *, batch, seq):
    rows = DN_CHUNKS_PER_STEP * CHUNK
    steps = seq // rows
    n = batch * seq
    consts = _dn_constants()

    def tok_spec(col_block):
        return pl.BlockSpec((rows, DN_WIDTH), lambda b, t: (b * steps + t, col_block))

    def full_spec(a):
        return pl.BlockSpec(a.shape, lambda b, t: (0,) * a.ndim)

    return pl.pallas_call(
        _deltanet_kernel,
        grid=(batch, steps),
        in_specs=[
            tok_spec(0), tok_spec(1), tok_spec(2), tok_spec(3),
            pl.BlockSpec((rows, LANES), lambda b, t: (b * steps + t, 0)),
            full_spec(conv_w), full_spec(alog_row), full_spec(dtb_row), full_spec(dn_norm_g),
        ] + [full_spec(a) for a in consts],
        out_specs=pl.BlockSpec((rows, DN_WIDTH), lambda b, t: (b * steps + t, 0)),
        out_shape=jax.ShapeDtypeStruct((n, DN_WIDTH), BF16),
        scratch_shapes=[
            pltpu.VMEM((DN_HEADS, DN_HEAD_DIM, DN_HEAD_DIM), F32),
            pltpu.VMEM((CONV_HALO, 3 * DN_WIDTH), BF16),
        ],
        compiler_params=pltpu.CompilerParams(
            dimension_semantics=("arbitrary", "arbitrary"), vmem_limit_bytes=VMEM_LIMIT),
        name="deltanet",
    )(big, big, big, big, ba, conv_w, alog_row, dtb_row, dn_norm_g, *consts)


def _mix_out_kernel(x_ref, mod_ref, pooled_ref, gp_ref, gd_ref, og_ref, pw_ref, ps_ref,
                    pp_ref, dp_ref, wo_ref, o_ref, ya_scr):
    for gi in range(len(POOL_WINDOWS)):
        cols = slice(gi * POOL_GROUP_DIM, (gi + 1) * POOL_GROUP_DIM)
        ya_g = _dot(pooled_ref[:, cols], pw_ref[gi]) * ps_ref[:, cols]
        ya_scr[:, cols] = ya_g.astype(BF16)

    ya = _dot(ya_scr[...], pp_ref[...])
    yb = _dot(og_ref[...], dp_ref[...])
    merged = gp_ref[...].astype(F32) * ya + gd_ref[...].astype(F32) * yb
    out = _dot(merged.astype(BF16), wo_ref[...])
    res_gate = mod_ref[5:6, :]
    o_ref[...] = x_ref[...] + res_gate * out


def _mix_out(x2d, mods, big, og, pool_w, pool_scale, pool_proj, dn_proj, w_out, *, batch, seq):
    steps = seq // TM_OUT
    n = batch * seq

    def row_map(b, t):
        return b * steps + t

    return pl.pallas_call(
        _mix_out_kernel,
        grid=(batch, steps),
        in_specs=[
            pl.BlockSpec((TM_OUT, D_MODEL), lambda b, t: (row_map(b, t), 0)),
            pl.BlockSpec((None, MOD_ROWS, D_MODEL), lambda b, t: (b, 0, 0)),
            pl.BlockSpec((TM_OUT, POOL_WIDTH), lambda b, t: (row_map(b, t), 12)),
            pl.BlockSpec((TM_OUT, D_MODEL), lambda b, t: (row_map(b, t), 4)),
            pl.BlockSpec((TM_OUT, D_MODEL), lambda b, t: (row_map(b, t), 5)),
            pl.BlockSpec((TM_OUT, DN_WIDTH), lambda b, t: (row_map(b, t), 0)),
            pl.BlockSpec((len(POOL_WINDOWS), POOL_GROUP_DIM, POOL_GROUP_DIM),
                         lambda b, t: (0, 0, 0)),
            pl.BlockSpec((1, POOL_WIDTH), lambda b, t: (0, 0)),
            pl.BlockSpec((POOL_WIDTH, D_MODEL), lambda b, t: (0, 0)),
            pl.BlockSpec((DN_WIDTH, D_MODEL), lambda b, t: (0, 0)),
            pl.BlockSpec((D_MODEL, D_MODEL), lambda b, t: (0, 0)),
        ],
        out_specs=pl.BlockSpec((TM_OUT, D_MODEL), lambda b, t: (row_map(b, t), 0)),
        out_shape=jax.ShapeDtypeStruct((n, D_MODEL), F32),
        scratch_shapes=[pltpu.VMEM((TM_OUT, POOL_WIDTH), BF16)],
        compiler_params=pltpu.CompilerParams(
            dimension_semantics=("arbitrary", "arbitrary"), vmem_limit_bytes=VMEM_LIMIT),
        name="mix_out",
    )(x2d, mods, big, big, big, og, pool_w, pool_scale, pool_proj, dn_proj, w_out)


def _layer(x2d, c_pad, ada_w, ada_b, norm_g, ffn1_w_in, ffn1_w_out, ffn2_w_in, ffn2_w_out,
           mix_w_in, conv_w, a_log, dt_bias, dn_norm_g, pool_w, pool_scale, pool_proj,
           dn_proj, mix_w_out, final_g, *, batch, seq, final):
    mod = _ada(c_pad, ada_w, ada_b[None, :])
    mods = mod[:batch].reshape(batch, 9, D_MODEL)
    mods = jnp.pad(mods, ((0, 0), (0, MOD_ROWS - 9), (0, 0)))

    fg = final_g[None, :]
    x2d = _ffn(x2d, mods, norm_g[0][None, :], ffn1_w_in.astype(BF16), ffn1_w_out.astype(BF16),
               fg, sub=0, final=False, seq=seq)

    o_q = POOL_WIDTH
    o_z = o_q + 3 * DN_WIDTH
    o_b = o_z + DN_WIDTH
    o_gp = o_b + GATE_LANES
    w_big = jnp.concatenate(
        [mix_w_in[:, o_q:o_b], mix_w_in[:, o_gp:o_gp + 2 * D_MODEL], mix_w_in[:, :POOL_WIDTH]],
        axis=1).astype(BF16)
    gate_pad = LANES - GATE_COPIES * GATE_LANES
    w_ba = jnp.pad(jnp.tile(mix_w_in[:, o_b:o_gp].astype(BF16), (1, GATE_COPIES)),
                   ((0, 0), (0, gate_pad)))
    big, ba = _mix_in(x2d, mods, norm_g[1][None, :], w_big, w_ba, batch=batch, seq=seq)

    def alpha_row(v):
        return jnp.pad(jnp.tile(jnp.pad(v, (DN_HEADS, 0)), GATE_COPIES), (0, gate_pad))[None, :]

    alog_row = alpha_row(a_log)
    dtb_row = alpha_row(dt_bias)
    og = _deltanet(big, ba, conv_w, alog_row, dtb_row, dn_norm_g[None, :],
                   batch=batch, seq=seq)

    x2d = _mix_out(x2d, mods, big, og, pool_w.astype(BF16), pool_scale[None, :],
                   pool_proj.astype(BF16), dn_proj.astype(BF16), mix_w_out.astype(BF16),
                   batch=batch, seq=seq)

    x2d = _ffn(x2d, mods, norm_g[2][None, :], ffn2_w_in.astype(BF16), ffn2_w_out.astype(BF16),
               fg, sub=2, final=final, seq=seq)
    return x2d


def kernel(x, c, ada_w, ada_b, norm_g, ffn1_w_in, ffn1_w_out, ffn2_w_in, ffn2_w_out, mix_w_in, conv_w, a_log, dt_bias, dn_norm_g, pool_w, pool_scale, pool_proj, dn_proj, mix_w_out, final_g):
    batch, seq, d = x.shape
    depth = ada_w.shape[0]
    x2d = x.reshape(batch * seq, d)
    c_pad = jnp.pad(c, ((0, SUBLANES - batch), (0, 0)))
    for l in range(depth):
        x2d = _layer(x2d, c_pad, ada_w[l], ada_b[l], norm_g[l], ffn1_w_in[l], ffn1_w_out[l],
                     ffn2_w_in[l], ffn2_w_out[l], mix_w_in[l], conv_w[l], a_log[l],
                     dt_bias[l], dn_norm_g[l], pool_w[l], pool_scale[l], pool_proj[l],
                     dn_proj[l], mix_w_out[l], final_g,
                     batch=batch, seq=seq, final=(l == depth - 1))
    return x2d.reshape(batch, seq, d)
```

```python
import functools

import jax
import jax.numpy as jnp
import numpy as np
from jax import lax
from jax.experimental import pallas as pl
from jax.experimental.pallas import tpu as pltpu

F32 = jnp.float32
BF16 = jnp.bfloat16

D_MODEL = 1024
POOL_WINDOWS = (2, 4, 8, 16)
POOL_GROUP_DIM = 128
POOL_WIDTH = 512
DN_HEAD_DIM = 128
DN_HEADS = 8
DN_WIDTH = 1024
CONV_WIDTH = 4
CHUNK = 64
FFN_HIDDEN = 2816
RMS_EPS = 1e-6
L2_EPS = 1e-6

LANES = 128
SUBLANES = 8
VMEM_LIMIT = 56 * 1024 * 1024

TM_FFN = 1024
TH_FFN = 256
TM_MIX = 512
TN_MIX = 512
EPI_ROWS = 64
DN_CHUNKS_PER_STEP = 8
GROUP_HEADS = 4
N_GROUPS = DN_HEADS // GROUP_HEADS
PACK = GROUP_HEADS * CHUNK
PACKED = DN_HEADS * CHUNK
TM_OUT = 1024
POOL_HALO = 16
CONV_HALO = 16
MOD_ROWS = 16
GATE_LANES = 2 * DN_HEADS
GATE_COPIES = 3


def _dot(a, b):
    return jnp.dot(a, b, preferred_element_type=F32)


def _dot_nt(a, b):
    return lax.dot_general(a, b, (((1,), (1,)), ((), ())), preferred_element_type=F32)


def _dot_tn(a, b):
    return lax.dot_general(a, b, (((0,), (0,)), ((), ())), preferred_element_type=F32)


def _sigmoid(x):
    return jax.nn.sigmoid(x)


def _silu(x):
    return x * _sigmoid(x)


def _rms_norm(x, g):
    ms = jnp.mean(x * x, axis=-1, keepdims=True)
    return (x * lax.rsqrt(ms + RMS_EPS)) * g


def _modulated_norm(x, g, mod_ref, sub):
    shift = mod_ref[3 * sub + 0:3 * sub + 1, :]
    scale = mod_ref[3 * sub + 1:3 * sub + 2, :]
    return _rms_norm(x, g) * (1.0 + scale) + shift


def _const_spec(shape):
    nd = len(shape)
    return pl.BlockSpec(shape, lambda *_: (0,) * nd, pipeline_mode=pl.Buffered(1))


def _ada_kernel(c_ref, w_ref, b_ref, o_ref):
    s = _silu(c_ref[...]).astype(BF16)
    o_ref[...] = _dot(s, w_ref[...].astype(BF16)) + b_ref[...]


def _ada(c_pad, ada_w, ada_b):
    n = ada_w.shape[1]
    tn = 1024
    return pl.pallas_call(
        _ada_kernel,
        grid=(n // tn,),
        in_specs=[
            pl.BlockSpec((SUBLANES, D_MODEL), lambda j: (0, 0)),
            pl.BlockSpec((D_MODEL, tn), lambda j: (0, j)),
            pl.BlockSpec((1, tn), lambda j: (0, j)),
        ],
        out_specs=pl.BlockSpec((SUBLANES, tn), lambda j: (0, j)),
        out_shape=jax.ShapeDtypeStruct((SUBLANES, n), F32),
        compiler_params=pltpu.CompilerParams(
            dimension_semantics=("arbitrary",), vmem_limit_bytes=VMEM_LIMIT),
        name="ada",
    )(c_pad, ada_w, ada_b)


def _ffn_kernel(x_ref, mod_ref, g_ref, wi_ref, wo_ref, fg_ref, o_ref,
                h_scr, acc_scr, *, sub, final):
    x = x_ref[...]
    h_scr[...] = _modulated_norm(x, g_ref[...], mod_ref, sub).astype(BF16)
    n_chunks = FFN_HIDDEN // TH_FFN
    for j in range(n_chunks):
        cols = slice(j * TH_FFN, (j + 1) * TH_FFN)
        up_cols = slice(FFN_HIDDEN + j * TH_FFN, FFN_HIDDEN + (j + 1) * TH_FFN)
        h = h_scr[...]
        gate = _dot(h, wi_ref[:, cols])
        up = _dot(h, wi_ref[:, up_cols])
        act = (_silu(gate) * up).astype(BF16)
        part = _dot(act, wo_ref[cols, :])
        if j == 0:
            acc_scr[...] = part
        else:
            acc_scr[...] += part
    res_gate = mod_ref[3 * sub + 2:3 * sub + 3, :]
    y = x_ref[...] + (0.5 * res_gate) * acc_scr[...]
    if final:
        y = _rms_norm(y, fg_ref[...])
    o_ref[...] = y


def _ffn(x2d, mods, norm_g, w_in, w_out, final_g, *, sub, final, seq):
    n = x2d.shape[0]
    tiles_per_seq = seq // TM_FFN
    return pl.pallas_call(
        functools.partial(_ffn_kernel, sub=sub, final=final),
        grid=(n // TM_FFN,),
        in_specs=[
            pl.BlockSpec((TM_FFN, D_MODEL), lambda i: (i, 0)),
            pl.BlockSpec((None, MOD_ROWS, D_MODEL), lambda i: (i // tiles_per_seq, 0, 0)),
            _const_spec((1, D_MODEL)),
            _const_spec((D_MODEL, 2 * FFN_HIDDEN)),
            _const_spec((FFN_HIDDEN, D_MODEL)),
            _const_spec((1, D_MODEL)),
        ],
        out_specs=pl.BlockSpec((TM_FFN, D_MODEL), lambda i: (i, 0)),
        out_shape=jax.ShapeDtypeStruct((n, D_MODEL), F32),
        scratch_shapes=[
            pltpu.VMEM((TM_FFN, D_MODEL), BF16),
            pltpu.VMEM((TM_FFN, D_MODEL), F32),
        ],
        compiler_params=pltpu.CompilerParams(
            dimension_semantics=("arbitrary",), vmem_limit_bytes=VMEM_LIMIT),
        name="ffn%d" % sub,
    )(x2d, mods, norm_g, w_in, w_out, final_g)


def _mix_in_kernel(x_ref, mod_ref, g_ref, w_ref, wba_ref, wtail_ref, o_ref, ba_ref,
                   h_scr, phalo):
    t = pl.program_id(1)

    @pl.when(t == 0)
    def _():
        phalo[...] = jnp.zeros(phalo.shape, F32)

    h_scr[...] = _modulated_norm(x_ref[...], g_ref[...], mod_ref, 1).astype(BF16)
    ba_ref[...] = _dot(h_scr[...], wba_ref[...])

    sub_pool = lax.broadcasted_iota(jnp.int32, (SUBLANES, POOL_GROUP_DIM), 0)

    def shift_rows(xb, s, sub):
        n = xb.shape[0] // SUBLANES
        rots = [pltpu.roll(xb[SUBLANES * k:SUBLANES * (k + 1)], s, axis=0) for k in range(n)]
        return jnp.concatenate(
            [jnp.where(sub < s, rots[k - 1], rots[k]) for k in range(1, n)], axis=0)

    def pool_block(acc, out0, gi, r0):
        win = POOL_WINDOWS[gi]
        lanes = slice(gi * POOL_GROUP_DIM, (gi + 1) * POOL_GROUP_DIM)
        gl = slice(out0 + gi * POOL_GROUP_DIM, out0 + (gi + 1) * POOL_GROUP_DIM)
        if r0 == 0:
            xb = jnp.concatenate([phalo[:, lanes], acc[0:EPI_ROWS, lanes]], axis=0)
            phalo[:, lanes] = acc[TM_MIX - POOL_HALO:, lanes]
        else:
            xb = acc[r0 - POOL_HALO:r0 + EPI_ROWS, lanes]
        x0 = xb[POOL_HALO:]
        wsum = xb[SUBLANES:]
        if win > SUBLANES:
            wsum = wsum + xb[:-SUBLANES]
        lag = 1
        while lag < min(win, SUBLANES):
            prev = jnp.concatenate([xb[:SUBLANES], wsum], axis=0)
            wsum = wsum + shift_rows(prev, lag, sub_pool)
            lag *= 2
        wsum = wsum[SUBLANES:]
        pos = (t * TM_MIX + r0 + 1
               + lax.broadcasted_iota(jnp.int32, (EPI_ROWS, 1), 0)).astype(F32)
        pooled = wsum / jnp.minimum(pos, float(win)) - x0
        o_ref[r0:r0 + EPI_ROWS, gl] = pooled.astype(BF16)

    def project(col0):
        return _dot(h_scr[...], w_ref[:, col0:col0 + TN_MIX])

    acc = project(0)
    for gi in range(len(POOL_WINDOWS)):
        for r0 in range(0, TM_MIX, EPI_ROWS):
            pool_block(acc, 6 * D_MODEL, gi, r0)

    for j in range(4 * DN_WIDTH // TN_MIX):
        cols = slice(j * TN_MIX, (j + 1) * TN_MIX)
        acc = project(POOL_WIDTH + j * TN_MIX)
        if (j * TN_MIX) // DN_WIDTH < 3:
            o_ref[:, cols] = acc.astype(BF16)
        else:
            o_ref[:, cols] = _silu(acc).astype(BF16)

    g0 = POOL_WIDTH + 4 * DN_WIDTH
    prev = project(g0)
    for j in range(2 * D_MODEL // TN_MIX):
        if j == 2 * D_MODEL // TN_MIX - 1:
            nxt = _dot(h_scr[...], wtail_ref[...])
        else:
            nxt = project(g0 + (j + 1) * TN_MIX)
        gates = jnp.concatenate([prev[:, GATE_LANES:], nxt[:, :GATE_LANES]], axis=1)
        o_ref[:, 4 * DN_WIDTH + j * TN_MIX:4 * DN_WIDTH + (j + 1) * TN_MIX] = (
            _sigmoid(gates).astype(BF16))
        prev = nxt


def _mix_in(x2d, mods, norm_g, w_mix, w_ba, w_tail, *, batch, seq):
    n = x2d.shape[0]
    n_out = 6 * D_MODEL + POOL_WIDTH
    steps = seq // TM_MIX
    assert TN_MIX == POOL_WIDTH and w_mix.shape[1] == n_out + GATE_LANES
    return pl.pallas_call(
        _mix_in_kernel,
        grid=(batch, steps),
        in_specs=[
            pl.BlockSpec((TM_MIX, D_MODEL), lambda b, t: (b * steps + t, 0)),
            pl.BlockSpec((None, MOD_ROWS, D_MODEL), lambda b, t: (b, 0, 0)),
            _const_spec((1, D_MODEL)),
            _const_spec(w_mix.shape),
            _const_spec((D_MODEL, LANES)),
            _const_spec((D_MODEL, LANES)),
        ],
        out_specs=[
            pl.BlockSpec((TM_MIX, n_out), lambda b, t: (b * steps + t, 0)),
            pl.BlockSpec((TM_MIX, LANES), lambda b, t: (b * steps + t, 0)),
        ],
        out_shape=[
            jax.ShapeDtypeStruct((n, n_out), BF16),
            jax.ShapeDtypeStruct((n, LANES), F32),
        ],
        scratch_shapes=[
            pltpu.VMEM((TM_MIX, D_MODEL), BF16),
            pltpu.VMEM((POOL_HALO, POOL_WIDTH), F32),
        ],
        compiler_params=pltpu.CompilerParams(
            dimension_semantics=("arbitrary", "arbitrary"), vmem_limit_bytes=VMEM_LIMIT),
        name="mix_in",
    )(x2d, mods, norm_g, w_mix, w_ba, w_tail)


def _split3(x):
    hi = x.astype(BF16)
    r = x - hi.astype(F32)
    mid = r.astype(BF16)
    lo = (r - mid.astype(F32)).astype(BF16)
    return hi, mid, lo


def _block_diag(blocks):
    n = len(blocks)
    zero = jnp.zeros_like(blocks[0])
    rows = [jnp.concatenate([blocks[i] if j == i else zero for j in range(n)], axis=1)
            for i in range(n)]
    return jnp.concatenate(rows, axis=0)


def _head_cols(h, width=DN_HEAD_DIM):
    return slice(h * width, (h + 1) * width)


def _l2n_heads(x, scale):
    outs = []
    for h in range(x.shape[1] // DN_HEAD_DIM):
        xh = x[:, _head_cols(h)]
        inv = lax.rsqrt(jnp.sum(xh * xh, axis=-1, keepdims=True) + L2_EPS)
        outs.append(xh * (inv * scale))
    return jnp.concatenate(outs, axis=1)


def _dn_constants():
    tri = np.tril(np.ones((CHUNK, CHUNK), np.float32))
    tri_blk = np.tile(np.kron(np.eye(DN_CHUNKS_PER_STEP, dtype=np.float32), tri), (1, 3))
    e64 = np.zeros((LANES, PACKED), np.float32)
    eb64 = np.zeros((LANES, PACKED), np.float32)
    e128 = np.zeros((LANES, DN_WIDTH), np.float32)
    eb128 = np.zeros((LANES, DN_WIDTH), np.float32)
    for h in range(DN_HEADS):
        e64[DN_HEADS + h, _head_cols(h, CHUNK)] = 1.0
        eb64[h, _head_cols(h, CHUNK)] = 1.0
        e128[DN_HEADS + h, _head_cols(h)] = 1.0
        eb128[h, _head_cols(h)] = 1.0
    eall = np.concatenate([eb64, e64, eb128, e128], axis=1)
    eall3 = np.zeros_like(eall)
    for p in range(GATE_COPIES):
        eall3[p * GATE_LANES:(p + 1) * GATE_LANES] = eall[:GATE_LANES]
    r = np.arange(CHUNK)[:, None]
    c = np.arange(PACKED)[None, :] % CHUNK
    u3 = np.tile((r <= c).astype(np.float32), (3, 1))
    rr = np.arange(PACK)[:, None] // CHUNK
    cc = np.arange(PACK)[None, :] // CHUNK
    bdm = (rr == cc).astype(np.float32)
    dshift = np.zeros(((CONV_WIDTH - 1) * CHUNK, CONV_HALO + CHUNK), np.float32)
    for s in range(1, CONV_WIDTH):
        dshift[(s - 1) * CHUNK + np.arange(CHUNK), CONV_HALO + np.arange(CHUNK) - s] = 1.0
    return tuple(jnp.asarray(a, BF16) for a in (tri_blk, e64, eall3, u3, bdm, dshift))


def _deltanet_kernel(q_ref, k_ref, v_ref, zg_ref, ba_ref, cw_ref, alog_ref, dtb_ref, ng_ref,
                     trib_ref, e64_ref, eall3_ref, u3_ref, bdm_ref, dshift_ref,
                     o_ref, s_scr, xhalo):
    t = pl.program_id(1)
    rows = DN_CHUNKS_PER_STEP * CHUNK

    @pl.when(t == 0)
    def _():
        s_scr[...] = jnp.zeros(s_scr.shape, F32)
        xhalo[...] = jnp.zeros(xhalo.shape, BF16)

    raw_refs = (q_ref, k_ref, v_ref)

    def conv_silu(c, sec):
        ref = raw_refs[sec]
        cols = slice(sec * DN_WIDTH, (sec + 1) * DN_WIDTH)
        if c == 0:
            xe = jnp.concatenate([xhalo[:, cols], ref[0:CHUNK, :]], axis=0)
        else:
            xe = ref[c * CHUNK - CONV_HALO:(c + 1) * CHUNK, :]
        shifted = _dot(dshift_ref[...], xe)
        y = cw_ref[CONV_WIDTH - 1:CONV_WIDTH, cols] * xe[CONV_HALO:].astype(F32)
        for s in range(1, CONV_WIDTH):
            tap = CONV_WIDTH - 1 - s
            y = y + cw_ref[tap:tap + 1, cols] * shifted[(s - 1) * CHUNK:s * CHUNK]
        return _silu(y)

    prow = lax.broadcasted_iota(jnp.int32, (CHUNK, PACKED), 0)
    pcol = lax.broadcasted_iota(jnp.int32, (CHUNK, PACKED), 1) & (CHUNK - 1)
    causal_t = prow >= pcol
    strict_t = prow > pcol
    eye_t = jnp.where(prow == pcol, 1.0, 0.0).astype(F32)
    gate_lane = lax.broadcasted_iota(jnp.int32, (rows, LANES), 1)
    is_beta_lane = (gate_lane & (GATE_LANES - 1)) < DN_HEADS

    neg_decay_rate = -jnp.exp(alog_ref[...])
    dt_bias = dtb_ref[...]
    norm_g = ng_ref[...]
    bdm = bdm_ref[...]

    def group_cols(g, width):
        return slice(g * GROUP_HEADS * width, (g + 1) * GROUP_HEADS * width)

    def head_blocks(x, g):
        return [x[:, _head_cols(g * GROUP_HEADS + i)] for i in range(GROUP_HEADS)]

    def packed_block_diag(wb):
        return jnp.concatenate([wb] * GROUP_HEADS, axis=0) * bdm

    chunks = range(DN_CHUNKS_PER_STEP)

    ba = ba_ref[...]
    beta = _sigmoid(ba)
    xa = ba + dt_bias
    softplus = jnp.maximum(xa, 0.0) + jnp.log1p(jnp.exp(-jnp.abs(xa)))
    g_log = neg_decay_rate * softplus

    gstack = jnp.concatenate(_split3(g_log), axis=0)
    gc = _dot(trib_ref[...], gstack)
    geb = _dot(gstack, e64_ref[...]).astype(BF16)
    bgc = jnp.where(is_beta_lane, beta, gc)
    hi = bgc.astype(BF16).astype(F32)
    rem = bgc - hi
    mid = rem.astype(BF16).astype(F32)
    piece = jnp.where(gate_lane < GATE_LANES, hi,
                      jnp.where(gate_lane < 2 * GATE_LANES, mid, rem - mid))
    x = _dot(piece.astype(BF16), eall3_ref[...])
    beta64 = x[:, :PACKED]
    gcol64 = x[:, PACKED:2 * PACKED]
    beta128 = x[:, 2 * PACKED:2 * PACKED + DN_WIDTH]
    gc128 = x[:, 2 * PACKED + DN_WIDTH:]
    ones_lhs = jnp.ones((2 * SUBLANES, 3 * CHUNK), BF16)

    st = []
    for c in chunks:
        rs = slice(c * CHUNK, (c + 1) * CHUNK)
        qn = _l2n_heads(conv_silu(c, 0), DN_HEAD_DIM ** -0.5)
        kn = _l2n_heads(conv_silu(c, 1), 1.0)
        pieces = [geb[p * rows + c * CHUNK:p * rows + (c + 1) * CHUNK] for p in range(3)]
        grow = _dot(ones_lhs, jnp.concatenate(pieces, axis=0) * u3_ref[...])[0:1]
        st.append(dict(qb=qn.astype(BF16), kb=kn.astype(BF16), qn=qn, kn=kn,
                       vc=conv_silu(c, 2),
                       gdiff=gcol64[rs] - grow,
                       beta64=beta64[rs], beta128=beta128[rs], gc128=gc128[rs]))

    for s in st:
        qk, kk = [], []
        for g in range(N_GROUPS):
            gs = group_cols(g, DN_HEAD_DIM)
            bk = _block_diag(head_blocks(s["kb"], g))
            r = _dot_nt(jnp.concatenate([s["qb"][:, gs], s["kb"][:, gs]], axis=0), bk)
            qk.append(r[:CHUNK])
            kk.append(r[CHUNK:])
        s["qk"] = jnp.concatenate(qk, axis=1)
        s["kk"] = jnp.concatenate(kk, axis=1)
        s.pop("qb")
        s.pop("kb")

    for s in st:
        decay = jnp.exp(jnp.where(causal_t, s.pop("gdiff"), -jnp.inf))
        a_mat = jnp.where(strict_t, s.pop("beta64") * s.pop("kk") * decay, 0.0)
        qkd = s.pop("qk") * decay
        s["qkd"] = [qkd[:, group_cols(g, CHUNK)].astype(BF16) for g in range(N_GROUPS)]
        s["w"] = [-a_mat[:, group_cols(g, CHUNK)] for g in range(N_GROUPS)]
        p0 = eye_t - a_mat
        s["p"] = [p0[:, group_cols(g, CHUNK)] for g in range(N_GROUPS)]

    for s in st:
        for g in range(N_GROUPS):
            wb = s["w"][g].astype(BF16)
            s["w"][g] = _dot(wb, packed_block_diag(wb))
    n = 4
    while n < CHUNK:
        for s in st:
            for g in range(N_GROUPS):
                wb = s["w"][g].astype(BF16)
                r = _dot(jnp.concatenate([wb, s["p"][g].astype(BF16)], axis=0),
                         packed_block_diag(wb))
                s["w"][g] = r[:CHUNK]
                s["p"][g] = s["p"][g] + r[CHUNK:]
        n *= 2
    for s in st:
        for g in range(N_GROUPS):
            s["p"][g] = s["p"][g] + _dot(s["p"][g].astype(BF16),
                                         packed_block_diag(s["w"][g].astype(BF16)))
        s.pop("w")

    for s in st:
        gc128_c = s.pop("gc128")
        beta128_c = s.pop("beta128")
        eg = jnp.exp(gc128_c)
        rv = (beta128_c * s.pop("vc")).astype(BF16)
        rk = ((beta128_c * eg) * s["kn"]).astype(BF16)
        us, ws = [], []
        for g in range(N_GROUPS):
            rhs = jnp.concatenate([_block_diag(head_blocks(rv, g)),
                                   _block_diag(head_blocks(rk, g))], axis=1)
            sol = _dot(s["p"][g].astype(BF16), rhs)
            us.append(sol[:, :GROUP_HEADS * DN_HEAD_DIM])
            ws.append(sol[:, GROUP_HEADS * DN_HEAD_DIM:])
        s.pop("p")
        s["u"] = jnp.concatenate(us, axis=1)
        w_all = jnp.concatenate(ws, axis=1)
        g_last = gc128_c[CHUNK - 1:CHUNK, :]
        q_dec = s.pop("qn") * eg
        s["wq"] = jnp.concatenate([w_all, q_dec], axis=0).astype(BF16)
        s["kdec"] = (s.pop("kn") * jnp.exp(g_last - gc128_c)).astype(BF16)
        s["sdecay"] = jnp.exp(g_last)

    state = [s_scr[h] for h in range(DN_HEADS)]
    pair = 2 * DN_HEAD_DIM
    for c in chunks:
        s = st[c]
        r0 = c * CHUNK
        ws = []
        for p in range(DN_HEADS // 2):
            rhs = _block_diag([state[2 * p].astype(BF16), state[2 * p + 1].astype(BF16)])
            ws.append(_dot(s["wq"][:, p * pair:(p + 1) * pair], rhs))
        ws = jnp.concatenate(ws, axis=1)
        v_new = (s["u"] - ws[:CHUNK]).astype(BF16)
        o = []
        for g in range(N_GROUPS):
            o.append(ws[CHUNK:, group_cols(g, DN_HEAD_DIM)]
                     + _dot(s["qkd"][g], _block_diag(head_blocks(v_new, g))))
        o = jnp.concatenate(o, axis=1)
        for p in range(DN_HEADS // 2):
            ps = slice(p * pair, (p + 1) * pair)
            upd = _dot_tn(s["kdec"][:, ps], v_new[:, ps])
            for i in range(2):
                h = 2 * p + i
                blk = slice(i * DN_HEAD_DIM, (i + 1) * DN_HEAD_DIM)
                state[h] = state[h] * s["sdecay"][:, _head_cols(h)] + upd[blk, blk]
        for h in range(DN_HEADS):
            hs = _head_cols(h)
            zg = zg_ref[r0:r0 + CHUNK, hs].astype(F32)
            o_ref[r0:r0 + CHUNK, hs] = (_rms_norm(o[:, hs], norm_g) * zg).astype(BF16)

    for h in range(DN_HEADS):
        s_scr[h] = state[h]
    for sec, ref in enumerate(raw_refs):
        xhalo[:, sec * DN_WIDTH:(sec + 1) * DN_WIDTH] = ref[rows - CONV_HALO:rows, :]


def _deltanet(big, ba, conv_w, alog_row, dtb_row, dn_norm_g, *, batch, seq):
    rows = DN_CHUNKS_PER_STEP * CHUNK
    steps = seq // rows
    n = batch * seq
    consts = _dn_constants()

    def tok_spec(col_block):
        return pl.BlockSpec((rows, DN_WIDTH), lambda b, t: (b * steps + t, col_block))

    def full_spec(a):
        return pl.BlockSpec(a.shape, lambda b, t: (0,) * a.ndim)

    return pl.pallas_call(
        _deltanet_kernel,
        grid=(batch, steps),
        in_specs=[
            tok_spec(0), tok_spec(1), tok_spec(2), tok_spec(3),
            pl.BlockSpec((rows, LANES), lambda b, t: (b * steps + t, 0)),
            full_spec(conv_w), full_spec(alog_row), full_spec(dtb_row), full_spec(dn_norm_g),
        ] + [full_spec(a) for a in consts],
        out_specs=pl.BlockSpec((rows, DN_WIDTH), lambda b, t: (b * steps + t, 0)),
        out_shape=jax.ShapeDtypeStruct((n, DN_WIDTH), BF16),
        scratch_shapes=[
            pltpu.VMEM((DN_HEADS, DN_HEAD_DIM, DN_HEAD_DIM), F32),
            pltpu.VMEM((CONV_HALO, 3 * DN_WIDTH), BF16),
        ],
        compiler_params=pltpu.CompilerParams(
            dimension_semantics=("arbitrary", "arbitrary"), vmem_limit_bytes=VMEM_LIMIT),
        name="deltanet",
    )(big, big, big, big, ba, conv_w, alog_row, dtb_row, dn_norm_g, *consts)


def _mix_out_kernel(x_ref, mod_ref, pooled_ref, gp_ref, gd_ref, og_ref, pw_ref, ps_ref,
                    pp_ref, dp_ref, wo_ref, o_ref, ya_scr):
    for gi in range(len(POOL_WINDOWS)):
        cols = slice(gi * POOL_GROUP_DIM, (gi + 1) * POOL_GROUP_DIM)
        ya_g = _dot(pooled_ref[:, cols], pw_ref[gi]) * ps_ref[:, cols]
        ya_scr[:, cols] = ya_g.astype(BF16)

    ya = _dot(ya_scr[...], pp_ref[...])
    yb = _dot(og_ref[...], dp_ref[...])
    merged = gp_ref[...].astype(F32) * ya + gd_ref[...].astype(F32) * yb
    out = _dot(merged.astype(BF16), wo_ref[...])
    res_gate = mod_ref[5:6, :]
    o_ref[...] = x_ref[...] + res_gate * out


def _mix_out(x2d, mods, big, og, pool_w, pool_scale, pool_proj, dn_proj, w_out, *, batch, seq):
    steps = seq // TM_OUT
    n = batch * seq

    def row_map(b, t):
        return b * steps + t

    return pl.pallas_call(
        _mix_out_kernel,
        grid=(batch, steps),
        in_specs=[
            pl.BlockSpec((TM_OUT, D_MODEL), lambda b, t: (row_map(b, t), 0)),
            pl.BlockSpec((None, MOD_ROWS, D_MODEL), lambda b, t: (b, 0, 0)),
            pl.BlockSpec((TM_OUT, POOL_WIDTH), lambda b, t: (row_map(b, t), 12)),
            pl.BlockSpec((TM_OUT, D_MODEL), lambda b, t: (row_map(b, t), 4)),
            pl.BlockSpec((TM_OUT, D_MODEL), lambda b, t: (row_map(b, t), 5)),
            pl.BlockSpec((TM_OUT, DN_WIDTH), lambda b, t: (row_map(b, t), 0)),
            pl.BlockSpec((len(POOL_WINDOWS), POOL_GROUP_DIM, POOL_GROUP_DIM),
                         lambda b, t: (0, 0, 0)),
            pl.BlockSpec((1, POOL_WIDTH), lambda b, t: (0, 0)),
            pl.BlockSpec((POOL_WIDTH, D_MODEL), lambda b, t: (0, 0)),
            pl.BlockSpec((DN_WIDTH, D_MODEL), lambda b, t: (0, 0)),
            pl.BlockSpec((D_MODEL, D_MODEL), lambda b, t: (0, 0)),
        ],
        out_specs=pl.BlockSpec((TM_OUT, D_MODEL), lambda b, t: (row_map(b, t), 0)),
        out_shape=jax.ShapeDtypeStruct((n, D_MODEL), F32),
        scratch_shapes=[pltpu.VMEM((TM_OUT, POOL_WIDTH), BF16)],
        compiler_params=pltpu.CompilerParams(
            dimension_semantics=("arbitrary", "arbitrary"), vmem_limit_bytes=VMEM_LIMIT),
        name="mix_out",
    )(x2d, mods, big, big, big, og, pool_w, pool_scale, pool_proj, dn_proj, w_out)


def _layer(x2d, c_pad, ada_w, ada_b, norm_g, ffn1_w_in, ffn1_w_out, ffn2_w_in, ffn2_w_out,
           mix_w_in, conv_w, a_log, dt_bias, dn_norm_g, pool_w, pool_scale, pool_proj,
           dn_proj, mix_w_out, final_g, *, batch, seq, final):
    mod = _ada(c_pad, ada_w, ada_b[None, :])
    mods = mod[:batch].reshape(batch, 9, D_MODEL)
    mods = jnp.pad(mods, ((0, 0), (0, MOD_ROWS - 9), (0, 0)))

    fg = final_g[None, :]
    x2d = _ffn(x2d, mods, norm_g[0][None, :], ffn1_w_in.astype(BF16), ffn1_w_out.astype(BF16),
               fg, sub=0, final=False, seq=seq)

    o_b = POOL_WIDTH + 4 * DN_WIDTH
    gate_pad = LANES - GATE_COPIES * GATE_LANES
    w_ba = jnp.pad(jnp.tile(mix_w_in[:, o_b:o_b + GATE_LANES].astype(BF16), (1, GATE_COPIES)),
                   ((0, 0), (0, gate_pad)))
    w_tail = jnp.pad(mix_w_in[:, -GATE_LANES:].astype(BF16), ((0, 0), (0, LANES - GATE_LANES)))
    big, ba = _mix_in(x2d, mods, norm_g[1][None, :], mix_w_in.astype(BF16), w_ba, w_tail,
                      batch=batch, seq=seq)

    def alpha_row(v):
        return jnp.pad(jnp.tile(jnp.pad(v, (DN_HEADS, 0)), GATE_COPIES), (0, gate_pad))[None, :]

    alog_row = alpha_row(a_log)
    dtb_row = alpha_row(dt_bias)
    og = _deltanet(big, ba, conv_w, alog_row, dtb_row, dn_norm_g[None, :],
                   batch=batch, seq=seq)

    x2d = _mix_out(x2d, mods, big, og, pool_w.astype(BF16), pool_scale[None, :],
                   pool_proj.astype(BF16), dn_proj.astype(BF16), mix_w_out.astype(BF16),
                   batch=batch, seq=seq)

    x2d = _ffn(x2d, mods, norm_g[2][None, :], ffn2_w_in.astype(BF16), ffn2_w_out.astype(BF16),
               fg, sub=2, final=final, seq=seq)
    return x2d


def kernel(x, c, ada_w, ada_b, norm_g, ffn1_w_in, ffn1_w_out, ffn2_w_in, ffn2_w_out, mix_w_in, conv_w, a_log, dt_bias, dn_norm_g, pool_w, pool_scale, pool_proj, dn_proj, mix_w_out, final_g):
    batch, seq, d = x.shape
    depth = ada_w.shape[0]
    x2d = x.reshape(batch * seq, d)
    c_pad = jnp.pad(c, ((0, SUBLANES - batch), (0, 0)))
    for l in range(depth):
        x2d = _layer(x2d, c_pad, ada_w[l], ada_b[l], norm_g[l], ffn1_w_in[l], ffn1_w_out[l],
                     ffn2_w_in[l], ffn2_w_out[l], mix_w_in[l], conv_w[l], a_log[l],
                     dt_bias[l], dn_norm_g[l], pool_w[l], pool_scale[l], pool_proj[l],
                     dn_proj[l], mix_w_out[l], final_g,
                     batch=batch, seq=seq, final=(l == depth - 1))
    return x2d.reshape(batch, seq, d)
```

```python
import functools

import jax
import jax.numpy as jnp
import numpy as np
from jax import lax
from jax.experimental import pallas as pl
from jax.experimental.pallas import tpu as pltpu

F32 = jnp.float32
BF16 = jnp.bfloat16

D_MODEL = 1024
POOL_WINDOWS = (2, 4, 8, 16)
POOL_GROUP_DIM = 128
POOL_WIDTH = 512
DN_HEAD_DIM = 128
DN_HEADS = 8
DN_WIDTH = 1024
CONV_WIDTH = 4
CHUNK = 64
FFN_HIDDEN = 2816
RMS_EPS = 1e-6
L2_EPS = 1e-6

LANES = 128
SUBLANES = 8
BF16_ROWS = 16
VMEM_LIMIT = 56 * 1024 * 1024

TM_FFN = 1024
TH_FFN = 256
TM_MIX = 512
TN_MIX = 512
EPI_ROWS = 64
DN_CHUNKS_PER_STEP = 8
DN_GROUP_CHUNKS = 4
GROUP_HEADS = 4
N_GROUPS = DN_HEADS // GROUP_HEADS
PACK = GROUP_HEADS * CHUNK
PACKED = DN_HEADS * CHUNK
TM_OUT = 1024
POOL_HALO = 16
CONV_HALO = 16
MOD_ROWS = 16
GATE_LANES = 2 * DN_HEADS
GATE_COPIES = 3


def _dot(a, b):
    return jnp.dot(a, b, preferred_element_type=F32)


def _dot_nt(a, b):
    return lax.dot_general(a, b, (((1,), (1,)), ((), ())), preferred_element_type=F32)


def _dot_tn(a, b):
    return lax.dot_general(a, b, (((0,), (0,)), ((), ())), preferred_element_type=F32)


def _sigmoid(x):
    return jax.nn.sigmoid(x)


def _silu(x):
    return x * _sigmoid(x)


def _rms_norm(x, g):
    ms = jnp.mean(x * x, axis=-1, keepdims=True)
    return (x * lax.rsqrt(ms + RMS_EPS)) * g


def _modulated_norm(x, g, mod_ref, sub):
    shift = mod_ref[3 * sub + 0:3 * sub + 1, :]
    scale = mod_ref[3 * sub + 1:3 * sub + 2, :]
    return _rms_norm(x, g) * (1.0 + scale) + shift


def _const_spec(shape):
    nd = len(shape)
    return pl.BlockSpec(shape, lambda *_: (0,) * nd, pipeline_mode=pl.Buffered(1))


def _side_cast_specs(arrays, n_steps, step_of):
    in_specs, out_specs, out_shapes = [], [], []
    for a in arrays:
        rows, cols = a.shape
        n_blocks = n_steps
        while rows % n_blocks or (rows // n_blocks) % BF16_ROWS:
            n_blocks //= 2
        repeat = n_steps // n_blocks
        spec = pl.BlockSpec((rows // n_blocks, cols),
                            lambda *idx, repeat=repeat: (step_of(*idx) // repeat, 0))
        in_specs.append(spec)
        out_specs.append(spec)
        out_shapes.append(jax.ShapeDtypeStruct(a.shape, BF16))
    return in_specs, out_specs, out_shapes


def _side_cast(in_refs, out_refs):
    for src_ref, dst_ref in zip(in_refs, out_refs):
        dst_ref[...] = src_ref[...].astype(BF16)


def _ada_kernel(c_ref, w_ref, b_ref, o_ref):
    s = _silu(c_ref[...]).astype(BF16)
    o_ref[...] = _dot(s, w_ref[...].astype(BF16)) + b_ref[...]


def _ada(c_pad, ada_w, ada_b):
    n = ada_w.shape[1]
    tn = 1024
    return pl.pallas_call(
        _ada_kernel,
        grid=(n // tn,),
        in_specs=[
            pl.BlockSpec((SUBLANES, D_MODEL), lambda j: (0, 0)),
            pl.BlockSpec((D_MODEL, tn), lambda j: (0, j)),
            pl.BlockSpec((1, tn), lambda j: (0, j)),
        ],
        out_specs=pl.BlockSpec((SUBLANES, tn), lambda j: (0, j)),
        out_shape=jax.ShapeDtypeStruct((SUBLANES, n), F32),
        compiler_params=pltpu.CompilerParams(
            dimension_semantics=("arbitrary",), vmem_limit_bytes=VMEM_LIMIT),
        name="ada",
    )(c_pad, ada_w, ada_b)


def _ffn_kernel(*refs, sub, final, n_cast):
    x_ref, mod_ref, g_ref, wi_ref, wo_ref, fg_ref = refs[:6]
    cast_in = refs[6:6 + n_cast]
    o_ref = refs[6 + n_cast]
    cast_out = refs[7 + n_cast:7 + 2 * n_cast]
    h_scr, acc_scr = refs[7 + 2 * n_cast:]
    _side_cast(cast_in, cast_out)
    x = x_ref[...]
    h_scr[...] = _modulated_norm(x, g_ref[...], mod_ref, sub).astype(BF16)
    n_chunks = FFN_HIDDEN // TH_FFN
    for j in range(n_chunks):
        cols = slice(j * TH_FFN, (j + 1) * TH_FFN)
        up_cols = slice(FFN_HIDDEN + j * TH_FFN, FFN_HIDDEN + (j + 1) * TH_FFN)
        h = h_scr[...]
        gate = _dot(h, wi_ref[:, cols])
        up = _dot(h, wi_ref[:, up_cols])
        act = (_silu(gate) * up).astype(BF16)
        part = _dot(act, wo_ref[cols, :])
        if j == 0:
            acc_scr[...] = part
        else:
            acc_scr[...] += part
    res_gate = mod_ref[3 * sub + 2:3 * sub + 3, :]
    y = x_ref[...] + (0.5 * res_gate) * acc_scr[...]
    if final:
        y = _rms_norm(y, fg_ref[...])
    o_ref[...] = y


def _ffn(x2d, mods, norm_g, w_in, w_out, final_g, *, sub, final, seq, cast=()):
    n = x2d.shape[0]
    tiles_per_seq = seq // TM_FFN
    cast_in, cast_out, cast_shapes = _side_cast_specs(cast, n // TM_FFN, lambda i: i)
    return pl.pallas_call(
        functools.partial(_ffn_kernel, sub=sub, final=final, n_cast=len(cast)),
        grid=(n // TM_FFN,),
        in_specs=[
            pl.BlockSpec((TM_FFN, D_MODEL), lambda i: (i, 0)),
            pl.BlockSpec((None, MOD_ROWS, D_MODEL), lambda i: (i // tiles_per_seq, 0, 0)),
            _const_spec((1, D_MODEL)),
            _const_spec((D_MODEL, 2 * FFN_HIDDEN)),
            _const_spec((FFN_HIDDEN, D_MODEL)),
            _const_spec((1, D_MODEL)),
        ] + cast_in,
        out_specs=[pl.BlockSpec((TM_FFN, D_MODEL), lambda i: (i, 0))] + cast_out,
        out_shape=[jax.ShapeDtypeStruct((n, D_MODEL), F32)] + cast_shapes,
        scratch_shapes=[
            pltpu.VMEM((TM_FFN, D_MODEL), BF16),
            pltpu.VMEM((TM_FFN, D_MODEL), F32),
        ],
        compiler_params=pltpu.CompilerParams(
            dimension_semantics=("arbitrary",), vmem_limit_bytes=VMEM_LIMIT),
        name="ffn%d" % sub,
    )(x2d, mods, norm_g, w_in, w_out, final_g, *cast)


def _mix_in_kernel(*refs, n_cast):
    x_ref, mod_ref, g_ref, w_ref, wba_ref, wtail_ref = refs[:6]
    cast_in = refs[6:6 + n_cast]
    o_ref, ba_ref = refs[6 + n_cast:8 + n_cast]
    cast_out = refs[8 + n_cast:8 + 2 * n_cast]
    h_scr, phalo = refs[8 + 2 * n_cast:]
    _mix_in_body(x_ref, mod_ref, g_ref, w_ref, wba_ref, wtail_ref, o_ref, ba_ref, h_scr, phalo)
    _side_cast(cast_in, cast_out)


def _mix_in_body(x_ref, mod_ref, g_ref, w_ref, wba_ref, wtail_ref, o_ref, ba_ref,
                 h_scr, phalo):
    t = pl.program_id(1)

    @pl.when(t == 0)
    def _():
        phalo[...] = jnp.zeros(phalo.shape, F32)

    h_scr[...] = _modulated_norm(x_ref[...], g_ref[...], mod_ref, 1).astype(BF16)
    ba_ref[...] = _dot(h_scr[...], wba_ref[...])

    sub_pool = lax.broadcasted_iota(jnp.int32, (SUBLANES, POOL_GROUP_DIM), 0)

    def shift_rows(xb, s, sub):
        n = xb.shape[0] // SUBLANES
        rots = [pltpu.roll(xb[SUBLANES * k:SUBLANES * (k + 1)], s, axis=0) for k in range(n)]
        return jnp.concatenate(
            [jnp.where(sub < s, rots[k - 1], rots[k]) for k in range(1, n)], axis=0)

    def pool_block(acc, out0, gi, r0):
        win = POOL_WINDOWS[gi]
        lanes = slice(gi * POOL_GROUP_DIM, (gi + 1) * POOL_GROUP_DIM)
        gl = slice(out0 + gi * POOL_GROUP_DIM, out0 + (gi + 1) * POOL_GROUP_DIM)
        if r0 == 0:
            xb = jnp.concatenate([phalo[:, lanes], acc[0:EPI_ROWS, lanes]], axis=0)
            phalo[:, lanes] = acc[TM_MIX - POOL_HALO:, lanes]
        else:
            xb = acc[r0 - POOL_HALO:r0 + EPI_ROWS, lanes]
        x0 = xb[POOL_HALO:]
        wsum = xb[SUBLANES:]
        if win > SUBLANES:
            wsum = wsum + xb[:-SUBLANES]
        lag = 1
        while lag < min(win, SUBLANES):
            prev = jnp.concatenate([xb[:SUBLANES], wsum], axis=0)
            wsum = wsum + shift_rows(prev, lag, sub_pool)
            lag *= 2
        wsum = wsum[SUBLANES:]
        pos = (t * TM_MIX + r0 + 1
               + lax.broadcasted_iota(jnp.int32, (EPI_ROWS, 1), 0)).astype(F32)
        pooled = wsum / jnp.minimum(pos, float(win)) - x0
        o_ref[r0:r0 + EPI_ROWS, gl] = pooled.astype(BF16)

    def project(col0):
        return _dot(h_scr[...], w_ref[:, col0:col0 + TN_MIX])

    acc = project(0)
    for gi in range(len(POOL_WINDOWS)):
        for r0 in range(0, TM_MIX, EPI_ROWS):
            pool_block(acc, 6 * D_MODEL, gi, r0)

    for j in range(4 * DN_WIDTH // TN_MIX):
        cols = slice(j * TN_MIX, (j + 1) * TN_MIX)
        acc = project(POOL_WIDTH + j * TN_MIX)
        if (j * TN_MIX) // DN_WIDTH < 3:
            o_ref[:, cols] = acc.astype(BF16)
        else:
            o_ref[:, cols] = _silu(acc).astype(BF16)

    g0 = POOL_WIDTH + 4 * DN_WIDTH
    prev = project(g0)
    for j in range(2 * D_MODEL // TN_MIX):
        if j == 2 * D_MODEL // TN_MIX - 1:
            nxt = _dot(h_scr[...], wtail_ref[...])
        else:
            nxt = project(g0 + (j + 1) * TN_MIX)
        gates = jnp.concatenate([prev[:, GATE_LANES:], nxt[:, :GATE_LANES]], axis=1)
        o_ref[:, 4 * DN_WIDTH + j * TN_MIX:4 * DN_WIDTH + (j + 1) * TN_MIX] = (
            _sigmoid(gates).astype(BF16))
        prev = nxt


def _mix_in(x2d, mods, norm_g, w_mix, w_ba, w_tail, *, batch, seq, cast=()):
    n = x2d.shape[0]
    n_out = 6 * D_MODEL + POOL_WIDTH
    steps = seq // TM_MIX
    assert TN_MIX == POOL_WIDTH and w_mix.shape[1] == n_out + GATE_LANES
    cast_in, cast_out, cast_shapes = _side_cast_specs(
        cast, batch * steps, lambda b, t: b * steps + t)
    return pl.pallas_call(
        functools.partial(_mix_in_kernel, n_cast=len(cast)),
        grid=(batch, steps),
        in_specs=[
            pl.BlockSpec((TM_MIX, D_MODEL), lambda b, t: (b * steps + t, 0)),
            pl.BlockSpec((None, MOD_ROWS, D_MODEL), lambda b, t: (b, 0, 0)),
            _const_spec((1, D_MODEL)),
            _const_spec(w_mix.shape),
            _const_spec((D_MODEL, LANES)),
            _const_spec((D_MODEL, LANES)),
        ] + cast_in,
        out_specs=[
            pl.BlockSpec((TM_MIX, n_out), lambda b, t: (b * steps + t, 0)),
            pl.BlockSpec((TM_MIX, LANES), lambda b, t: (b * steps + t, 0)),
        ] + cast_out,
        out_shape=[
            jax.ShapeDtypeStruct((n, n_out), BF16),
            jax.ShapeDtypeStruct((n, LANES), F32),
        ] + cast_shapes,
        scratch_shapes=[
            pltpu.VMEM((TM_MIX, D_MODEL), BF16),
            pltpu.VMEM((POOL_HALO, POOL_WIDTH), F32),
        ],
        compiler_params=pltpu.CompilerParams(
            dimension_semantics=("arbitrary", "arbitrary"), vmem_limit_bytes=VMEM_LIMIT),
        name="mix_in",
    )(x2d, mods, norm_g, w_mix, w_ba, w_tail, *cast)


def _split3(x):
    hi = x.astype(BF16)
    r = x - hi.astype(F32)
    mid = r.astype(BF16)
    lo = (r - mid.astype(F32)).astype(BF16)
    return hi, mid, lo


def _block_diag(blocks):
    n = len(blocks)
    zero = jnp.zeros_like(blocks[0])
    rows = [jnp.concatenate([blocks[i] if j == i else zero for j in range(n)], axis=1)
            for i in range(n)]
    return jnp.concatenate(rows, axis=0)


def _head_cols(h, width=DN_HEAD_DIM):
    return slice(h * width, (h + 1) * width)


def _l2n_heads(x, scale):
    outs = []
    for h in range(x.shape[1] // DN_HEAD_DIM):
        xh = x[:, _head_cols(h)]
        inv = lax.rsqrt(jnp.sum(xh * xh, axis=-1, keepdims=True) + L2_EPS)
        outs.append(xh * (inv * scale))
    return jnp.concatenate(outs, axis=1)


def _dn_constants():
    tri = np.tril(np.ones((CHUNK, CHUNK), np.float32))
    tri_blk = np.tile(np.kron(np.eye(DN_CHUNKS_PER_STEP, dtype=np.float32), tri), (1, 3))
    e64 = np.zeros((LANES, PACKED), np.float32)
    eb64 = np.zeros((LANES, PACKED), np.float32)
    e128 = np.zeros((LANES, DN_WIDTH), np.float32)
    eb128 = np.zeros((LANES, DN_WIDTH), np.float32)
    for h in range(DN_HEADS):
        e64[DN_HEADS + h, _head_cols(h, CHUNK)] = 1.0
        eb64[h, _head_cols(h, CHUNK)] = 1.0
        e128[DN_HEADS + h, _head_cols(h)] = 1.0
        eb128[h, _head_cols(h)] = 1.0
    eall = np.concatenate([eb64, e64, eb128, e128], axis=1)
    eall3 = np.zeros_like(eall)
    for p in range(GATE_COPIES):
        eall3[p * GATE_LANES:(p + 1) * GATE_LANES] = eall[:GATE_LANES]
    r = np.arange(CHUNK)[:, None]
    c = np.arange(PACKED)[None, :] % CHUNK
    u3 = np.tile((r <= c).astype(np.float32), (3, 1))
    rr = np.arange(PACK)[:, None] // CHUNK
    cc = np.arange(PACK)[None, :] // CHUNK
    bdm = (rr == cc).astype(np.float32)
    dshift = np.zeros(((CONV_WIDTH - 1) * CHUNK, CONV_HALO + CHUNK), np.float32)
    for s in range(1, CONV_WIDTH):
        dshift[(s - 1) * CHUNK + np.arange(CHUNK), CONV_HALO + np.arange(CHUNK) - s] = 1.0
    return tuple(jnp.asarray(a, BF16) for a in (tri_blk, e64, eall3, u3, bdm, dshift))


def _deltanet_kernel(q_ref, k_ref, v_ref, zg_ref, ba_ref, cw_ref, alog_ref, dtb_ref, ng_ref,
                     trib_ref, e64_ref, eall3_ref, u3_ref, bdm_ref, dshift_ref,
                     o_ref, s_scr, xhalo):
    t = pl.program_id(1)
    rows = DN_CHUNKS_PER_STEP * CHUNK

    @pl.when(t == 0)
    def _():
        s_scr[...] = jnp.zeros(s_scr.shape, F32)
        xhalo[...] = jnp.zeros(xhalo.shape, BF16)

    raw_refs = (q_ref, k_ref, v_ref)

    def conv_silu(c, sec):
        ref = raw_refs[sec]
        cols = slice(sec * DN_WIDTH, (sec + 1) * DN_WIDTH)
        if c == 0:
            xe = jnp.concatenate([xhalo[:, cols], ref[0:CHUNK, :]], axis=0)
        else:
            xe = ref[c * CHUNK - CONV_HALO:(c + 1) * CHUNK, :]
        shifted = _dot(dshift_ref[...], xe)
        y = cw_ref[CONV_WIDTH - 1:CONV_WIDTH, cols] * xe[CONV_HALO:].astype(F32)
        for s in range(1, CONV_WIDTH):
            tap = CONV_WIDTH - 1 - s
            y = y + cw_ref[tap:tap + 1, cols] * shifted[(s - 1) * CHUNK:s * CHUNK]
        return _silu(y)

    prow = lax.broadcasted_iota(jnp.int32, (CHUNK, PACKED), 0)
    pcol = lax.broadcasted_iota(jnp.int32, (CHUNK, PACKED), 1) & (CHUNK - 1)
    causal_t = prow >= pcol
    strict_t = prow > pcol
    eye_t = jnp.where(prow == pcol, 1.0, 0.0).astype(F32)
    gate_lane = lax.broadcasted_iota(jnp.int32, (rows, LANES), 1)
    is_beta_lane = (gate_lane & (GATE_LANES - 1)) < DN_HEADS

    neg_decay_rate = -jnp.exp(alog_ref[...])
    dt_bias = dtb_ref[...]
    norm_g = ng_ref[...]
    bdm = bdm_ref[...]

    def group_cols(g, width):
        return slice(g * GROUP_HEADS * width, (g + 1) * GROUP_HEADS * width)

    def head_blocks(x, g):
        return [x[:, _head_cols(g * GROUP_HEADS + i)] for i in range(GROUP_HEADS)]

    def packed_block_diag(wb):
        return jnp.concatenate([wb] * GROUP_HEADS, axis=0) * bdm

    ba = ba_ref[...]
    beta = _sigmoid(ba)
    xa = ba + dt_bias
    softplus = jnp.maximum(xa, 0.0) + jnp.log1p(jnp.exp(-jnp.abs(xa)))
    g_log = neg_decay_rate * softplus

    gstack = jnp.concatenate(_split3(g_log), axis=0)
    gc = _dot(trib_ref[...], gstack)
    geb = _dot(gstack, e64_ref[...]).astype(BF16)
    bgc = jnp.where(is_beta_lane, beta, gc)
    hi = bgc.astype(BF16).astype(F32)
    rem = bgc - hi
    mid = rem.astype(BF16).astype(F32)
    piece = jnp.where(gate_lane < GATE_LANES, hi,
                      jnp.where(gate_lane < 2 * GATE_LANES, mid, rem - mid))
    x = _dot(piece.astype(BF16), eall3_ref[...])
    beta64 = x[:, :PACKED]
    gcol64 = x[:, PACKED:2 * PACKED]
    beta128 = x[:, 2 * PACKED:2 * PACKED + DN_WIDTH]
    gc128 = x[:, 2 * PACKED + DN_WIDTH:]
    ones_lhs = jnp.ones((2 * SUBLANES, 3 * CHUNK), BF16)

    st = {}

    def prepare(cs):
        for c in cs:
            rs = slice(c * CHUNK, (c + 1) * CHUNK)
            qn = _l2n_heads(conv_silu(c, 0), DN_HEAD_DIM ** -0.5)
            kn = _l2n_heads(conv_silu(c, 1), 1.0)
            pieces = [geb[p * rows + c * CHUNK:p * rows + (c + 1) * CHUNK] for p in range(3)]
            grow = _dot(ones_lhs, jnp.concatenate(pieces, axis=0) * u3_ref[...])[0:1]
            st[c] = dict(qb=qn.astype(BF16), kb=kn.astype(BF16), qn=qn, kn=kn,
                         vc=conv_silu(c, 2),
                         gdiff=gcol64[rs] - grow,
                         beta64=beta64[rs], beta128=beta128[rs], gc128=gc128[rs])
            yield
        sts = [st[c] for c in cs]

        for s in sts:
            qk, kk = [], []
            for g in range(N_GROUPS):
                gs = group_cols(g, DN_HEAD_DIM)
                bk = _block_diag(head_blocks(s["kb"], g))
                r = _dot_nt(jnp.concatenate([s["qb"][:, gs], s["kb"][:, gs]], axis=0), bk)
                qk.append(r[:CHUNK])
                kk.append(r[CHUNK:])
            s["qk"] = jnp.concatenate(qk, axis=1)
            s["kk"] = jnp.concatenate(kk, axis=1)
            s.pop("qb")
            s.pop("kb")
        yield

        for s in sts:
            decay = jnp.exp(jnp.where(causal_t, s.pop("gdiff"), -jnp.inf))
            a_mat = jnp.where(strict_t, s.pop("beta64") * s.pop("kk") * decay, 0.0)
            qkd = s.pop("qk") * decay
            s["qkd"] = [qkd[:, group_cols(g, CHUNK)].astype(BF16) for g in range(N_GROUPS)]
            s["w"] = [-a_mat[:, group_cols(g, CHUNK)] for g in range(N_GROUPS)]
            p0 = eye_t - a_mat
            s["p"] = [p0[:, group_cols(g, CHUNK)] for g in range(N_GROUPS)]
        yield

        for s in sts:
            for g in range(N_GROUPS):
                wb = s["w"][g].astype(BF16)
                s["w"][g] = _dot(wb, packed_block_diag(wb))
        yield
        n = 4
        while n < CHUNK:
            for s in sts:
                for g in range(N_GROUPS):
                    wb = s["w"][g].astype(BF16)
                    r = _dot(jnp.concatenate([wb, s["p"][g].astype(BF16)], axis=0),
                             packed_block_diag(wb))
                    s["w"][g] = r[:CHUNK]
                    s["p"][g] = s["p"][g] + r[CHUNK:]
            yield
            n *= 2
        for s in sts:
            for g in range(N_GROUPS):
                s["p"][g] = s["p"][g] + _dot(s["p"][g].astype(BF16),
                                             packed_block_diag(s["w"][g].astype(BF16)))
            s.pop("w")
        yield

        for s in sts:
            gc128_c = s.pop("gc128")
            beta128_c = s.pop("beta128")
            eg = jnp.exp(gc128_c)
            rv = (beta128_c * s.pop("vc")).astype(BF16)
            rk = ((beta128_c * eg) * s["kn"]).astype(BF16)
            us, ws = [], []
            for g in range(N_GROUPS):
                rhs = jnp.concatenate([_block_diag(head_blocks(rv, g)),
                                       _block_diag(head_blocks(rk, g))], axis=1)
                sol = _dot(s["p"][g].astype(BF16), rhs)
                us.append(sol[:, :GROUP_HEADS * DN_HEAD_DIM])
                ws.append(sol[:, GROUP_HEADS * DN_HEAD_DIM:])
            s.pop("p")
            s["u"] = jnp.concatenate(us, axis=1)
            w_all = jnp.concatenate(ws, axis=1)
            g_last = gc128_c[CHUNK - 1:CHUNK, :]
            q_dec = s.pop("qn") * eg
            s["wq"] = jnp.concatenate([w_all, q_dec], axis=0).astype(BF16)
            s["kdec"] = (s.pop("kn") * jnp.exp(g_last - gc128_c)).astype(BF16)
            s["sdecay"] = jnp.exp(g_last)
            yield

    state = [s_scr[h] for h in range(DN_HEADS)]
    pair = 2 * DN_HEAD_DIM

    def recurrence(cs):
        for c in cs:
            s = st.pop(c)
            r0 = c * CHUNK
            ws = []
            for p in range(DN_HEADS // 2):
                rhs = _block_diag([state[2 * p].astype(BF16), state[2 * p + 1].astype(BF16)])
                ws.append(_dot(s["wq"][:, p * pair:(p + 1) * pair], rhs))
            yield
            ws = jnp.concatenate(ws, axis=1)
            v_new = (s["u"] - ws[:CHUNK]).astype(BF16)
            o = []
            for g in range(N_GROUPS):
                o.append(ws[CHUNK:, group_cols(g, DN_HEAD_DIM)]
                         + _dot(s["qkd"][g], _block_diag(head_blocks(v_new, g))))
            upds = []
            for p in range(DN_HEADS // 2):
                ps = slice(p * pair, (p + 1) * pair)
                upds.append(_dot_tn(s["kdec"][:, ps], v_new[:, ps]))
            yield
            for p in range(DN_HEADS // 2):
                for i in range(2):
                    h = 2 * p + i
                    blk = slice(i * DN_HEAD_DIM, (i + 1) * DN_HEAD_DIM)
                    state[h] = state[h] * s["sdecay"][:, _head_cols(h)] + upds[p][blk, blk]
            o = jnp.concatenate(o, axis=1)
            for h in range(DN_HEADS):
                hs = _head_cols(h)
                zg = zg_ref[r0:r0 + CHUNK, hs].astype(F32)
                o_ref[r0:r0 + CHUNK, hs] = (_rms_norm(o[:, hs], norm_g) * zg).astype(BF16)

    def interleave(streams, shares):
        live = [True] * len(streams)
        while any(live):
            for i, stream in enumerate(streams):
                for _ in range(shares[i]):
                    if live[i]:
                        try:
                            next(stream)
                        except StopIteration:
                            live[i] = False

    groups = [list(range(i, i + DN_GROUP_CHUNKS))
              for i in range(0, DN_CHUNKS_PER_STEP, DN_GROUP_CHUNKS)]
    interleave([prepare(groups[0])], [1])
    for prev_group, group in zip(groups[:-1], groups[1:]):
        interleave([recurrence(prev_group), prepare(group)], [1, 3])
    interleave([recurrence(groups[-1])], [1])

    for h in range(DN_HEADS):
        s_scr[h] = state[h]
    for sec, ref in enumerate(raw_refs):
        xhalo[:, sec * DN_WIDTH:(sec + 1) * DN_WIDTH] = ref[rows - CONV_HALO:rows, :]


def _deltanet(big, ba, conv_w, alog_row, dtb_row, dn_norm_g, *, batch, seq):
    rows = DN_CHUNKS_PER_STEP * CHUNK
    steps = seq // rows
    n = batch * seq
    consts = _dn_constants()

    def tok_spec(col_block):
        return pl.BlockSpec((rows, DN_WIDTH), lambda b, t: (b * steps + t, col_block))

    def full_spec(a):
        return pl.BlockSpec(a.shape, lambda b, t: (0,) * a.ndim)

    return pl.pallas_call(
        _deltanet_kernel,
        grid=(batch, steps),
        in_specs=[
            tok_spec(0), tok_spec(1), tok_spec(2), tok_spec(3),
            pl.BlockSpec((rows, LANES), lambda b, t: (b * steps + t, 0)),
            full_spec(conv_w), full_spec(alog_row), full_spec(dtb_row), full_spec(dn_norm_g),
        ] + [full_spec(a) for a in consts],
        out_specs=pl.BlockSpec((rows, DN_WIDTH), lambda b, t: (b * steps + t, 0)),
        out_shape=jax.ShapeDtypeStruct((n, DN_WIDTH), BF16),
        scratch_shapes=[
            pltpu.VMEM((DN_HEADS, DN_HEAD_DIM, DN_HEAD_DIM), F32),
            pltpu.VMEM((CONV_HALO, 3 * DN_WIDTH), BF16),
        ],
        compiler_params=pltpu.CompilerParams(
            dimension_semantics=("arbitrary", "arbitrary"), vmem_limit_bytes=VMEM_LIMIT),
        name="deltanet",
    )(big, big, big, big, ba, conv_w, alog_row, dtb_row, dn_norm_g, *consts)


def _mix_out_kernel(x_ref, mod_ref, pooled_ref, gp_ref, gd_ref, og_ref, pw_ref, ps_ref,
                    pp_ref, dp_ref, wo_ref, o_ref, ya_scr):
    for gi in range(len(POOL_WINDOWS)):
        cols = slice(gi * POOL_GROUP_DIM, (gi + 1) * POOL_GROUP_DIM)
        ya_g = _dot(pooled_ref[:, cols], pw_ref[gi]) * ps_ref[:, cols]
        ya_scr[:, cols] = ya_g.astype(BF16)

    ya = _dot(ya_scr[...], pp_ref[...])
    yb = _dot(og_ref[...], dp_ref[...])
    merged = gp_ref[...].astype(F32) * ya + gd_ref[...].astype(F32) * yb
    out = _dot(merged.astype(BF16), wo_ref[...])
    res_gate = mod_ref[5:6, :]
    o_ref[...] = x_ref[...] + res_gate * out


def _mix_out(x2d, mods, big, og, pool_w, pool_scale, pool_proj, dn_proj, w_out, *, batch, seq):
    steps = seq // TM_OUT
    n = batch * seq

    def row_map(b, t):
        return b * steps + t

    return pl.pallas_call(
        _mix_out_kernel,
        grid=(batch, steps),
        in_specs=[
            pl.BlockSpec((TM_OUT, D_MODEL), lambda b, t: (row_map(b, t), 0)),
            pl.BlockSpec((None, MOD_ROWS, D_MODEL), lambda b, t: (b, 0, 0)),
            pl.BlockSpec((TM_OUT, POOL_WIDTH), lambda b, t: (row_map(b, t), 12)),
            pl.BlockSpec((TM_OUT, D_MODEL), lambda b, t: (row_map(b, t), 4)),
            pl.BlockSpec((TM_OUT, D_MODEL), lambda b, t: (row_map(b, t), 5)),
            pl.BlockSpec((TM_OUT, DN_WIDTH), lambda b, t: (row_map(b, t), 0)),
            pl.BlockSpec((len(POOL_WINDOWS), POOL_GROUP_DIM, POOL_GROUP_DIM),
                         lambda b, t: (0, 0, 0)),
            pl.BlockSpec((1, POOL_WIDTH), lambda b, t: (0, 0)),
            pl.BlockSpec((POOL_WIDTH, D_MODEL), lambda b, t: (0, 0)),
            pl.BlockSpec((DN_WIDTH, D_MODEL), lambda b, t: (0, 0)),
            pl.BlockSpec((D_MODEL, D_MODEL), lambda b, t: (0, 0)),
        ],
        out_specs=pl.BlockSpec((TM_OUT, D_MODEL), lambda b, t: (row_map(b, t), 0)),
        out_shape=jax.ShapeDtypeStruct((n, D_MODEL), F32),
        scratch_shapes=[pltpu.VMEM((TM_OUT, POOL_WIDTH), BF16)],
        compiler_params=pltpu.CompilerParams(
            dimension_semantics=("arbitrary", "arbitrary"), vmem_limit_bytes=VMEM_LIMIT),
        name="mix_out",
    )(x2d, mods, big, big, big, og, pool_w, pool_scale, pool_proj, dn_proj, w_out)


def _layer(x2d, c_pad, ada_w, ada_b, norm_g, ffn1_w_in, ffn1_w_out, ffn2_w_in, ffn2_w_out,
           mix_w_in, conv_w, a_log, dt_bias, dn_norm_g, pool_w, pool_scale, pool_proj,
           dn_proj, mix_w_out, final_g, *, batch, seq, final):
    mod = _ada(c_pad, ada_w, ada_b[None, :])
    mods = mod[:batch].reshape(batch, 9, D_MODEL)
    mods = jnp.pad(mods, ((0, 0), (0, MOD_ROWS - 9), (0, 0)))

    fg = final_g[None, :]
    x2d, = _ffn(x2d, mods, norm_g[0][None, :], ffn1_w_in.astype(BF16), ffn1_w_out.astype(BF16),
                fg, sub=0, final=False, seq=seq)
    w_mix = mix_w_in.astype(BF16)

    o_b = POOL_WIDTH + 4 * DN_WIDTH
    gate_pad = LANES - GATE_COPIES * GATE_LANES
    w_ba = jnp.pad(jnp.tile(w_mix[:, o_b:o_b + GATE_LANES], (1, GATE_COPIES)),
                   ((0, 0), (0, gate_pad)))
    w_tail = jnp.pad(w_mix[:, -GATE_LANES:], ((0, 0), (0, LANES - GATE_LANES)))
    n_pool = len(POOL_WINDOWS) * POOL_GROUP_DIM
    big, ba, w2_in, w2_out, w_dn, w_mo, w_pp, w_pw = _mix_in(
        x2d, mods, norm_g[1][None, :], w_mix, w_ba, w_tail, batch=batch, seq=seq,
        cast=(ffn2_w_in, ffn2_w_out, dn_proj, mix_w_out, pool_proj,
              pool_w.reshape(n_pool, POOL_GROUP_DIM)))

    def alpha_row(v):
        return jnp.pad(jnp.tile(jnp.pad(v, (DN_HEADS, 0)), GATE_COPIES), (0, gate_pad))[None, :]

    alog_row = alpha_row(a_log)
    dtb_row = alpha_row(dt_bias)
    og = _deltanet(big, ba, conv_w, alog_row, dtb_row, dn_norm_g[None, :],
                   batch=batch, seq=seq)

    x2d = _mix_out(x2d, mods, big, og, w_pw.reshape(pool_w.shape), pool_scale[None, :],
                   w_pp, w_dn, w_mo, batch=batch, seq=seq)

    x2d, = _ffn(x2d, mods, norm_g[2][None, :], w2_in, w2_out, fg, sub=2, final=final, seq=seq)
    return x2d


def kernel(x, c, ada_w, ada_b, norm_g, ffn1_w_in, ffn1_w_out, ffn2_w_in, ffn2_w_out, mix_w_in, conv_w, a_log, dt_bias, dn_norm_g, pool_w, pool_scale, pool_proj, dn_proj, mix_w_out, final_g):
    batch, seq, d = x.shape
    depth = ada_w.shape[0]
    x2d = x.reshape(batch * seq, d)
    c_pad = jnp.pad(c, ((0, SUBLANES - batch), (0, 0)))
    for l in range(depth):
        x2d = _layer(x2d, c_pad, ada_w[l], ada_b[l], norm_g[l], ffn1_w_in[l], ffn1_w_out[l],
                     ffn2_w_in[l], ffn2_w_out[l], mix_w_in[l], conv_w[l], a_log[l],
                     dt_bias[l], dn_norm_g[l], pool_w[l], pool_scale[l], pool_proj[l],
                     dn_proj[l], mix_w_out[l], final_g,
                     batch=batch, seq=seq, final=(l == depth - 1))
    return x2d.reshape(batch, seq, d)
```

```python
import functools

import jax
import jax.numpy as jnp
import numpy as np
from jax import lax
from jax.experimental import pallas as pl
from jax.experimental.pallas import tpu as pltpu

F32 = jnp.float32
BF16 = jnp.bfloat16

D_MODEL = 1024
POOL_WINDOWS = (2, 4, 8, 16)
POOL_GROUP_DIM = 128
POOL_WIDTH = 512
DN_HEAD_DIM = 128
DN_HEADS = 8
DN_WIDTH = 1024
CONV_WIDTH = 4
CHUNK = 64
FFN_HIDDEN = 2816
RMS_EPS = 1e-6
L2_EPS = 1e-6

LANES = 128
SUBLANES = 8
BF16_ROWS = 16
VMEM_LIMIT = 56 * 1024 * 1024

TM_FFN = 1024
TH_FFN = 256
TM_MIX = 512
TN_MIX = 512
EPI_ROWS = 64
DN_CHUNKS_PER_STEP = 8
DN_GROUP_CHUNKS = 4
GROUP_HEADS = 4
N_GROUPS = DN_HEADS // GROUP_HEADS
PACK = GROUP_HEADS * CHUNK
PACKED = DN_HEADS * CHUNK
TM_OUT = 1024
POOL_HALO = 16
CONV_HALO = 16
MOD_ROWS = 16
GATE_LANES = 2 * DN_HEADS
GATE_COPIES = 3


def _dot(a, b):
    return jnp.dot(a, b, preferred_element_type=F32)


def _dot_nt(a, b):
    return lax.dot_general(a, b, (((1,), (1,)), ((), ())), preferred_element_type=F32)


def _dot_tn(a, b):
    return lax.dot_general(a, b, (((0,), (0,)), ((), ())), preferred_element_type=F32)


def _sigmoid(x):
    return jax.nn.sigmoid(x)


def _silu(x):
    return x * _sigmoid(x)


def _rms_norm(x, g):
    ms = jnp.mean(x * x, axis=-1, keepdims=True)
    return (x * lax.rsqrt(ms + RMS_EPS)) * g


def _modulated_norm(x, g, mod_ref, sub):
    shift = mod_ref[3 * sub + 0:3 * sub + 1, :]
    scale = mod_ref[3 * sub + 1:3 * sub + 2, :]
    return _rms_norm(x, g) * (1.0 + scale) + shift


def _const_spec(shape):
    nd = len(shape)
    return pl.BlockSpec(shape, lambda *_: (0,) * nd, pipeline_mode=pl.Buffered(1))


def _side_cast_specs(arrays, n_steps, step_of):
    in_specs, out_specs, out_shapes = [], [], []
    for a in arrays:
        rows, cols = a.shape
        n_blocks = n_steps
        while rows % n_blocks or (rows // n_blocks) % BF16_ROWS:
            n_blocks //= 2
        repeat = n_steps // n_blocks
        spec = pl.BlockSpec((rows // n_blocks, cols),
                            lambda *idx, repeat=repeat: (step_of(*idx) // repeat, 0))
        in_specs.append(spec)
        out_specs.append(spec)
        out_shapes.append(jax.ShapeDtypeStruct(a.shape, BF16))
    return in_specs, out_specs, out_shapes


def _side_cast(in_refs, out_refs):
    for src_ref, dst_ref in zip(in_refs, out_refs):
        dst_ref[...] = src_ref[...].astype(BF16)


def _ada_kernel(c_ref, w_ref, b_ref, o_ref):
    s = _silu(c_ref[...]).astype(BF16)
    o_ref[...] = _dot(s, w_ref[...].astype(BF16)) + b_ref[...]


def _ada(c_pad, ada_w, ada_b):
    n = ada_w.shape[1]
    tn = n // 4
    return pl.pallas_call(
        _ada_kernel,
        grid=(n // tn,),
        in_specs=[
            pl.BlockSpec((SUBLANES, D_MODEL), lambda j: (0, 0)),
            pl.BlockSpec((D_MODEL, tn), lambda j: (0, j)),
            pl.BlockSpec((1, tn), lambda j: (0, j)),
        ],
        out_specs=pl.BlockSpec((SUBLANES, tn), lambda j: (0, j)),
        out_shape=jax.ShapeDtypeStruct((SUBLANES, n), F32),
        compiler_params=pltpu.CompilerParams(
            dimension_semantics=("arbitrary",), vmem_limit_bytes=VMEM_LIMIT),
        name="ada",
    )(c_pad, ada_w, ada_b)


def _ffn_kernel(*refs, sub, final, n_cast):
    x_ref, mod_ref, g_ref, wi_ref, wo_ref, fg_ref = refs[:6]
    cast_in = refs[6:6 + n_cast]
    o_ref = refs[6 + n_cast]
    cast_out = refs[7 + n_cast:7 + 2 * n_cast]
    h_scr, acc_scr = refs[7 + 2 * n_cast:]
    _side_cast(cast_in, cast_out)
    x = x_ref[...]
    h_scr[...] = _modulated_norm(x, g_ref[...], mod_ref, sub).astype(BF16)
    n_chunks = FFN_HIDDEN // TH_FFN
    for j in range(n_chunks):
        cols = slice(j * TH_FFN, (j + 1) * TH_FFN)
        up_cols = slice(FFN_HIDDEN + j * TH_FFN, FFN_HIDDEN + (j + 1) * TH_FFN)
        h = h_scr[...]
        gate = _dot(h, wi_ref[:, cols])
        up = _dot(h, wi_ref[:, up_cols])
        act = (_silu(gate) * up).astype(BF16)
        part = _dot(act, wo_ref[cols, :])
        if j == 0:
            acc_scr[...] = part
        else:
            acc_scr[...] += part
    res_gate = mod_ref[3 * sub + 2:3 * sub + 3, :]
    y = x_ref[...] + (0.5 * res_gate) * acc_scr[...]
    if final:
        y = _rms_norm(y, fg_ref[...])
    o_ref[...] = y


def _ffn(x2d, mods, norm_g, w_in, w_out, final_g, *, sub, final, seq, cast=()):
    n = x2d.shape[0]
    tiles_per_seq = seq // TM_FFN
    cast_in, cast_out, cast_shapes = _side_cast_specs(cast, n // TM_FFN, lambda i: i)
    return pl.pallas_call(
        functools.partial(_ffn_kernel, sub=sub, final=final, n_cast=len(cast)),
        grid=(n // TM_FFN,),
        in_specs=[
            pl.BlockSpec((TM_FFN, D_MODEL), lambda i: (i, 0)),
            pl.BlockSpec((None, MOD_ROWS, D_MODEL), lambda i: (i // tiles_per_seq, 0, 0)),
            _const_spec((1, D_MODEL)),
            _const_spec((D_MODEL, 2 * FFN_HIDDEN)),
            _const_spec((FFN_HIDDEN, D_MODEL)),
            _const_spec((1, D_MODEL)),
        ] + cast_in,
        out_specs=[pl.BlockSpec((TM_FFN, D_MODEL), lambda i: (i, 0))] + cast_out,
        out_shape=[jax.ShapeDtypeStruct((n, D_MODEL), F32)] + cast_shapes,
        scratch_shapes=[
            pltpu.VMEM((TM_FFN, D_MODEL), BF16),
            pltpu.VMEM((TM_FFN, D_MODEL), F32),
        ],
        compiler_params=pltpu.CompilerParams(
            dimension_semantics=("arbitrary",), vmem_limit_bytes=VMEM_LIMIT),
        name="ffn%d" % sub,
    )(x2d, mods, norm_g, w_in, w_out, final_g, *cast)


def _mix_in_kernel(*refs, n_cast):
    x_ref, mod_ref, g_ref, w_ref, wtail_ref = refs[:5]
    cast_in = refs[5:5 + n_cast]
    o_ref, ba_ref = refs[5 + n_cast:7 + n_cast]
    cast_out = refs[7 + n_cast:7 + 2 * n_cast]
    h_scr, phalo = refs[7 + 2 * n_cast:]
    _mix_in_body(x_ref, mod_ref, g_ref, w_ref, wtail_ref, o_ref, ba_ref, h_scr, phalo)
    _side_cast(cast_in, cast_out)


def _mix_in_body(x_ref, mod_ref, g_ref, w_ref, wtail_ref, o_ref, ba_ref, h_scr, phalo):
    t = pl.program_id(1)

    @pl.when(t == 0)
    def _():
        phalo[...] = jnp.zeros(phalo.shape, F32)

    h_scr[...] = _modulated_norm(x_ref[...], g_ref[...], mod_ref, 1).astype(BF16)

    sub_pool = lax.broadcasted_iota(jnp.int32, (SUBLANES, POOL_GROUP_DIM), 0)

    def shift_rows(xb, s, sub):
        n = xb.shape[0] // SUBLANES
        rots = [pltpu.roll(xb[SUBLANES * k:SUBLANES * (k + 1)], s, axis=0) for k in range(n)]
        return jnp.concatenate(
            [jnp.where(sub < s, rots[k - 1], rots[k]) for k in range(1, n)], axis=0)

    def pool_block(acc, out0, gi, r0):
        win = POOL_WINDOWS[gi]
        lanes = slice(gi * POOL_GROUP_DIM, (gi + 1) * POOL_GROUP_DIM)
        gl = slice(out0 + gi * POOL_GROUP_DIM, out0 + (gi + 1) * POOL_GROUP_DIM)
        if r0 == 0:
            xb = jnp.concatenate([phalo[:, lanes], acc[0:EPI_ROWS, lanes]], axis=0)
            phalo[:, lanes] = acc[TM_MIX - POOL_HALO:, lanes]
        else:
            xb = acc[r0 - POOL_HALO:r0 + EPI_ROWS, lanes]
        x0 = xb[POOL_HALO:]
        wsum = xb[SUBLANES:]
        if win > SUBLANES:
            wsum = wsum + xb[:-SUBLANES]
        lag = 1
        while lag < min(win, SUBLANES):
            prev = jnp.concatenate([xb[:SUBLANES], wsum], axis=0)
            wsum = wsum + shift_rows(prev, lag, sub_pool)
            lag *= 2
        wsum = wsum[SUBLANES:]
        pos = (t * TM_MIX + r0 + 1
               + lax.broadcasted_iota(jnp.int32, (EPI_ROWS, 1), 0)).astype(F32)
        pooled = wsum / jnp.minimum(pos, float(win)) - x0
        o_ref[r0:r0 + EPI_ROWS, gl] = pooled.astype(BF16)

    def project(col0):
        return _dot(h_scr[...], w_ref[:, col0:col0 + TN_MIX])

    acc = project(0)
    for gi in range(len(POOL_WINDOWS)):
        for r0 in range(0, TM_MIX, EPI_ROWS):
            pool_block(acc, 6 * D_MODEL, gi, r0)

    for j in range(4 * DN_WIDTH // TN_MIX):
        cols = slice(j * TN_MIX, (j + 1) * TN_MIX)
        acc = project(POOL_WIDTH + j * TN_MIX)
        if (j * TN_MIX) // DN_WIDTH < 3:
            o_ref[:, cols] = acc.astype(BF16)
        else:
            o_ref[:, cols] = _silu(acc).astype(BF16)

    g0 = POOL_WIDTH + 4 * DN_WIDTH
    prev = project(g0)
    slab = prev[:, :LANES]
    lane = lax.broadcasted_iota(jnp.int32, slab.shape, 1)
    copies = jnp.zeros_like(slab)
    for p in reversed(range(GATE_COPIES)):
        shifted = slab if p == 0 else pltpu.roll(slab, p * GATE_LANES, axis=1)
        copies = jnp.where(lane < (p + 1) * GATE_LANES, shifted, copies)
    ba_ref[...] = copies
    for j in range(2 * D_MODEL // TN_MIX):
        if j == 2 * D_MODEL // TN_MIX - 1:
            nxt = _dot(h_scr[...], wtail_ref[...])
        else:
            nxt = project(g0 + (j + 1) * TN_MIX)
        gates = jnp.concatenate([prev[:, GATE_LANES:], nxt[:, :GATE_LANES]], axis=1)
        o_ref[:, 4 * DN_WIDTH + j * TN_MIX:4 * DN_WIDTH + (j + 1) * TN_MIX] = (
            _sigmoid(gates).astype(BF16))
        prev = nxt


def _mix_in(x2d, mods, norm_g, w_mix, w_tail, *, batch, seq, cast=()):
    n = x2d.shape[0]
    n_out = 6 * D_MODEL + POOL_WIDTH
    steps = seq // TM_MIX
    assert TN_MIX == POOL_WIDTH and w_mix.shape[1] == n_out + GATE_LANES
    cast_in, cast_out, cast_shapes = _side_cast_specs(
        cast, batch * steps, lambda b, t: b * steps + t)
    return pl.pallas_call(
        functools.partial(_mix_in_kernel, n_cast=len(cast)),
        grid=(batch, steps),
        in_specs=[
            pl.BlockSpec((TM_MIX, D_MODEL), lambda b, t: (b * steps + t, 0)),
            pl.BlockSpec((None, MOD_ROWS, D_MODEL), lambda b, t: (b, 0, 0)),
            _const_spec((1, D_MODEL)),
            _const_spec(w_mix.shape),
            _const_spec((D_MODEL, LANES)),
        ] + cast_in,
        out_specs=[
            pl.BlockSpec((TM_MIX, n_out), lambda b, t: (b * steps + t, 0)),
            pl.BlockSpec((TM_MIX, LANES), lambda b, t: (b * steps + t, 0)),
        ] + cast_out,
        out_shape=[
            jax.ShapeDtypeStruct((n, n_out), BF16),
            jax.ShapeDtypeStruct((n, LANES), F32),
        ] + cast_shapes,
        scratch_shapes=[
            pltpu.VMEM((TM_MIX, D_MODEL), BF16),
            pltpu.VMEM((POOL_HALO, POOL_WIDTH), F32),
        ],
        compiler_params=pltpu.CompilerParams(
            dimension_semantics=("arbitrary", "arbitrary"), vmem_limit_bytes=VMEM_LIMIT),
        name="mix_in",
    )(x2d, mods, norm_g, w_mix, w_tail, *cast)


def _split3(x):
    hi = x.astype(BF16)
    r = x - hi.astype(F32)
    mid = r.astype(BF16)
    lo = (r - mid.astype(F32)).astype(BF16)
    return hi, mid, lo


def _block_diag(blocks):
    n = len(blocks)
    zero = jnp.zeros_like(blocks[0])
    rows = [jnp.concatenate([blocks[i] if j == i else zero for j in range(n)], axis=1)
            for i in range(n)]
    return jnp.concatenate(rows, axis=0)


def _head_cols(h, width=DN_HEAD_DIM):
    return slice(h * width, (h + 1) * width)


def _l2n_heads(x, scale):
    outs = []
    for h in range(x.shape[1] // DN_HEAD_DIM):
        xh = x[:, _head_cols(h)]
        inv = lax.rsqrt(jnp.sum(xh * xh, axis=-1, keepdims=True) + L2_EPS)
        outs.append(xh * (inv * scale))
    return jnp.concatenate(outs, axis=1)


def _dn_constants():
    tri = np.tril(np.ones((CHUNK, CHUNK), np.float32))
    tri_blk = np.tile(np.kron(np.eye(DN_CHUNKS_PER_STEP, dtype=np.float32), tri), (1, 3))
    e64 = np.zeros((LANES, PACKED), np.float32)
    eb64 = np.zeros((LANES, PACKED), np.float32)
    e128 = np.zeros((LANES, DN_WIDTH), np.float32)
    eb128 = np.zeros((LANES, DN_WIDTH), np.float32)
    for h in range(DN_HEADS):
        e64[DN_HEADS + h, _head_cols(h, CHUNK)] = 1.0
        eb64[h, _head_cols(h, CHUNK)] = 1.0
        e128[DN_HEADS + h, _head_cols(h)] = 1.0
        eb128[h, _head_cols(h)] = 1.0
    eall = np.concatenate([eb64, e64, eb128, e128], axis=1)
    eall3 = np.zeros_like(eall)
    for p in range(GATE_COPIES):
        eall3[p * GATE_LANES:(p + 1) * GATE_LANES] = eall[:GATE_LANES]
    r = np.arange(CHUNK)[:, None]
    c = np.arange(PACKED)[None, :] % CHUNK
    u3 = np.tile((r <= c).astype(np.float32), (3, 1))
    rr = np.arange(PACK)[:, None] // CHUNK
    cc = np.arange(PACK)[None, :] // CHUNK
    bdm = (rr == cc).astype(np.float32)
    dshift = np.zeros(((CONV_WIDTH - 1) * CHUNK, CONV_HALO + CHUNK), np.float32)
    for s in range(1, CONV_WIDTH):
        dshift[(s - 1) * CHUNK + np.arange(CHUNK), CONV_HALO + np.arange(CHUNK) - s] = 1.0
    return tuple(jnp.asarray(a, BF16) for a in (tri_blk, e64, eall3, u3, bdm, dshift))


def _deltanet_kernel(q_ref, k_ref, v_ref, zg_ref, ba_ref, cw_ref, alog_ref, dtb_ref, ng_ref,
                     trib_ref, e64_ref, eall3_ref, u3_ref, bdm_ref, dshift_ref,
                     o_ref, s_scr, xhalo):
    t = pl.program_id(1)
    rows = DN_CHUNKS_PER_STEP * CHUNK

    @pl.when(t == 0)
    def _():
        s_scr[...] = jnp.zeros(s_scr.shape, F32)
        xhalo[...] = jnp.zeros(xhalo.shape, BF16)

    raw_refs = (q_ref, k_ref, v_ref)

    def conv_silu(c, sec):
        ref = raw_refs[sec]
        cols = slice(sec * DN_WIDTH, (sec + 1) * DN_WIDTH)
        if c == 0:
            xe = jnp.concatenate([xhalo[:, cols], ref[0:CHUNK, :]], axis=0)
        else:
            xe = ref[c * CHUNK - CONV_HALO:(c + 1) * CHUNK, :]
        shifted = _dot(dshift_ref[...], xe)
        y = cw_ref[CONV_WIDTH - 1:CONV_WIDTH, cols] * xe[CONV_HALO:].astype(F32)
        for s in range(1, CONV_WIDTH):
            tap = CONV_WIDTH - 1 - s
            y = y + cw_ref[tap:tap + 1, cols] * shifted[(s - 1) * CHUNK:s * CHUNK]
        return _silu(y)

    prow = lax.broadcasted_iota(jnp.int32, (CHUNK, PACKED), 0)
    pcol = lax.broadcasted_iota(jnp.int32, (CHUNK, PACKED), 1) & (CHUNK - 1)
    causal_t = prow >= pcol
    strict_t = prow > pcol
    eye_t = jnp.where(prow == pcol, 1.0, 0.0).astype(F32)
    gate_lane = lax.broadcasted_iota(jnp.int32, (rows, LANES), 1)
    is_beta_lane = (gate_lane & (GATE_LANES - 1)) < DN_HEADS

    neg_decay_rate = -jnp.exp(alog_ref[...])
    dt_bias = dtb_ref[...]
    norm_g = ng_ref[...]
    bdm = bdm_ref[...]

    def group_cols(g, width):
        return slice(g * GROUP_HEADS * width, (g + 1) * GROUP_HEADS * width)

    def head_blocks(x, g):
        return [x[:, _head_cols(g * GROUP_HEADS + i)] for i in range(GROUP_HEADS)]

    def packed_block_diag(wb):
        return jnp.concatenate([wb] * GROUP_HEADS, axis=0) * bdm

    ba = ba_ref[...]
    beta = _sigmoid(ba)
    xa = ba + dt_bias
    softplus = jnp.maximum(xa, 0.0) + jnp.log1p(jnp.exp(-jnp.abs(xa)))
    g_log = neg_decay_rate * softplus

    gstack = jnp.concatenate(_split3(g_log), axis=0)
    gc = _dot(trib_ref[...], gstack)
    geb = _dot(gstack, e64_ref[...]).astype(BF16)
    bgc = jnp.where(is_beta_lane, beta, gc)
    hi = bgc.astype(BF16).astype(F32)
    rem = bgc - hi
    mid = rem.astype(BF16).astype(F32)
    piece = jnp.where(gate_lane < GATE_LANES, hi,
                      jnp.where(gate_lane < 2 * GATE_LANES, mid, rem - mid))
    x = _dot(piece.astype(BF16), eall3_ref[...])
    beta64 = x[:, :PACKED]
    gcol64 = x[:, PACKED:2 * PACKED]
    beta128 = x[:, 2 * PACKED:2 * PACKED + DN_WIDTH]
    gc128 = x[:, 2 * PACKED + DN_WIDTH:]
    ones_lhs = jnp.ones((2 * SUBLANES, 3 * CHUNK), BF16)

    st = {}

    def prepare(cs):
        for c in cs:
            rs = slice(c * CHUNK, (c + 1) * CHUNK)
            qn = _l2n_heads(conv_silu(c, 0), DN_HEAD_DIM ** -0.5)
            kn = _l2n_heads(conv_silu(c, 1), 1.0)
            pieces = [geb[p * rows + c * CHUNK:p * rows + (c + 1) * CHUNK] for p in range(3)]
            grow = _dot(ones_lhs, jnp.concatenate(pieces, axis=0) * u3_ref[...])[0:1]
            st[c] = dict(qb=qn.astype(BF16), kb=kn.astype(BF16), qn=qn, kn=kn,
                         vc=conv_silu(c, 2),
                         gdiff=gcol64[rs] - grow,
                         beta64=beta64[rs], beta128=beta128[rs], gc128=gc128[rs])
            yield
        sts = [st[c] for c in cs]

        for s in sts:
            qk, kk = [], []
            for g in range(N_GROUPS):
                gs = group_cols(g, DN_HEAD_DIM)
                bk = _block_diag(head_blocks(s["kb"], g))
                r = _dot_nt(jnp.concatenate([s["qb"][:, gs], s["kb"][:, gs]], axis=0), bk)
                qk.append(r[:CHUNK])
                kk.append(r[CHUNK:])
            s["qk"] = jnp.concatenate(qk, axis=1)
            s["kk"] = jnp.concatenate(kk, axis=1)
            s.pop("qb")
            s.pop("kb")
        yield

        for s in sts:
            decay = jnp.exp(jnp.where(causal_t, s.pop("gdiff"), -jnp.inf))
            a_mat = jnp.where(strict_t, s.pop("beta64") * s.pop("kk") * decay, 0.0)
            qkd = s.pop("qk") * decay
            s["qkd"] = [qkd[:, group_cols(g, CHUNK)].astype(BF16) for g in range(N_GROUPS)]
            s["w"] = [-a_mat[:, group_cols(g, CHUNK)] for g in range(N_GROUPS)]
            p0 = eye_t - a_mat
            s["p"] = [p0[:, group_cols(g, CHUNK)] for g in range(N_GROUPS)]
        yield

        for s in sts:
            for g in range(N_GROUPS):
                wb = s["w"][g].astype(BF16)
                s["w"][g] = _dot(wb, packed_block_diag(wb))
        yield
        n = 4
        while n < CHUNK:
            for s in sts:
                for g in range(N_GROUPS):
                    wb = s["w"][g].astype(BF16)
                    r = _dot(jnp.concatenate([wb, s["p"][g].astype(BF16)], axis=0),
                             packed_block_diag(wb))
                    s["w"][g] = r[:CHUNK]
                    s["p"][g] = s["p"][g] + r[CHUNK:]
            yield
            n *= 2
        for s in sts:
            for g in range(N_GROUPS):
                s["p"][g] = s["p"][g] + _dot(s["p"][g].astype(BF16),
                                             packed_block_diag(s["w"][g].astype(BF16)))
            s.pop("w")
        yield

        for s in sts:
            gc128_c = s.pop("gc128")
            beta128_c = s.pop("beta128")
            eg = jnp.exp(gc128_c)
            rv = (beta128_c * s.pop("vc")).astype(BF16)
            rk = ((beta128_c * eg) * s["kn"]).astype(BF16)
            us, ws = [], []
            for g in range(N_GROUPS):
                rhs = jnp.concatenate([_block_diag(head_blocks(rv, g)),
                                       _block_diag(head_blocks(rk, g))], axis=1)
                sol = _dot(s["p"][g].astype(BF16), rhs)
                us.append(sol[:, :GROUP_HEADS * DN_HEAD_DIM])
                ws.append(sol[:, GROUP_HEADS * DN_HEAD_DIM:])
            s.pop("p")
            s["u"] = jnp.concatenate(us, axis=1)
            w_all = jnp.concatenate(ws, axis=1)
            g_last = gc128_c[CHUNK - 1:CHUNK, :]
            q_dec = s.pop("qn") * eg
            s["wq"] = jnp.concatenate([w_all, q_dec], axis=0).astype(BF16)
            s["kdec"] = (s.pop("kn") * jnp.exp(g_last - gc128_c)).astype(BF16)
            s["sdecay"] = jnp.exp(g_last)
            yield

    state = [s_scr[h] for h in range(DN_HEADS)]
    pair = 2 * DN_HEAD_DIM

    def recurrence(cs):
        for c in cs:
            s = st.pop(c)
            r0 = c * CHUNK
            ws = []
            for p in range(DN_HEADS // 2):
                rhs = _block_diag([state[2 * p].astype(BF16), state[2 * p + 1].astype(BF16)])
                ws.append(_dot(s["wq"][:, p * pair:(p + 1) * pair], rhs))
            yield
            ws = jnp.concatenate(ws, axis=1)
            v_new = (s["u"] - ws[:CHUNK]).astype(BF16)
            o = []
            for g in range(N_GROUPS):
                o.append(ws[CHUNK:, group_cols(g, DN_HEAD_DIM)]
                         + _dot(s["qkd"][g], _block_diag(head_blocks(v_new, g))))
            upds = []
            for p in range(DN_HEADS // 2):
                ps = slice(p * pair, (p + 1) * pair)
                upds.append(_dot_tn(s["kdec"][:, ps], v_new[:, ps]))
            yield
            for p in range(DN_HEADS // 2):
                for i in range(2):
                    h = 2 * p + i
                    blk = slice(i * DN_HEAD_DIM, (i + 1) * DN_HEAD_DIM)
                    state[h] = state[h] * s["sdecay"][:, _head_cols(h)] + upds[p][blk, blk]
            o = jnp.concatenate(o, axis=1)
            for h in range(DN_HEADS):
                hs = _head_cols(h)
                zg = zg_ref[r0:r0 + CHUNK, hs].astype(F32)
                o_ref[r0:r0 + CHUNK, hs] = (_rms_norm(o[:, hs], norm_g) * zg).astype(BF16)

    def interleave(streams, shares):
        live = [True] * len(streams)
        while any(live):
            for i, stream in enumerate(streams):
                for _ in range(shares[i]):
                    if live[i]:
                        try:
                            next(stream)
                        except StopIteration:
                            live[i] = False

    groups = [list(range(i, i + DN_GROUP_CHUNKS))
              for i in range(0, DN_CHUNKS_PER_STEP, DN_GROUP_CHUNKS)]
    interleave([prepare(groups[0])], [1])
    for prev_group, group in zip(groups[:-1], groups[1:]):
        interleave([recurrence(prev_group), prepare(group)], [1, 3])
    interleave([recurrence(groups[-1])], [1])

    for h in range(DN_HEADS):
        s_scr[h] = state[h]
    for sec, ref in enumerate(raw_refs):
        xhalo[:, sec * DN_WIDTH:(sec + 1) * DN_WIDTH] = ref[rows - CONV_HALO:rows, :]


def _deltanet(big, ba, conv_w, alog_row, dtb_row, dn_norm_g, *, batch, seq):
    rows = DN_CHUNKS_PER_STEP * CHUNK
    steps = seq // rows
    n = batch * seq
    consts = _dn_constants()

    def tok_spec(col_block):
        return pl.BlockSpec((rows, DN_WIDTH), lambda b, t: (b * steps + t, col_block))

    def full_spec(a):
        return pl.BlockSpec(a.shape, lambda b, t: (0,) * a.ndim)

    return pl.pallas_call(
        _deltanet_kernel,
        grid=(batch, steps),
        in_specs=[
            tok_spec(0), tok_spec(1), tok_spec(2), tok_spec(3),
            pl.BlockSpec((rows, LANES), lambda b, t: (b * steps + t, 0)),
            full_spec(conv_w), full_spec(alog_row), full_spec(dtb_row), full_spec(dn_norm_g),
        ] + [full_spec(a) for a in consts],
        out_specs=pl.BlockSpec((rows, DN_WIDTH), lambda b, t: (b * steps + t, 0)),
        out_shape=jax.ShapeDtypeStruct((n, DN_WIDTH), BF16),
        scratch_shapes=[
            pltpu.VMEM((DN_HEADS, DN_HEAD_DIM, DN_HEAD_DIM), F32),
            pltpu.VMEM((CONV_HALO, 3 * DN_WIDTH), BF16),
        ],
        compiler_params=pltpu.CompilerParams(
            dimension_semantics=("arbitrary", "arbitrary"), vmem_limit_bytes=VMEM_LIMIT),
        name="deltanet",
    )(big, big, big, big, ba, conv_w, alog_row, dtb_row, dn_norm_g, *consts)


def _mix_out_kernel(x_ref, mod_ref, pooled_ref, gp_ref, gd_ref, og_ref, pw_ref, ps_ref,
                    pp_ref, dp_ref, wo_ref, o_ref, ya_scr):
    for gi in range(len(POOL_WINDOWS)):
        cols = slice(gi * POOL_GROUP_DIM, (gi + 1) * POOL_GROUP_DIM)
        ya_g = _dot(pooled_ref[:, cols], pw_ref[gi]) * ps_ref[:, cols]
        ya_scr[:, cols] = ya_g.astype(BF16)

    ya = _dot(ya_scr[...], pp_ref[...])
    yb = _dot(og_ref[...], dp_ref[...])
    merged = gp_ref[...].astype(F32) * ya + gd_ref[...].astype(F32) * yb
    out = _dot(merged.astype(BF16), wo_ref[...])
    res_gate = mod_ref[5:6, :]
    o_ref[...] = x_ref[...] + res_gate * out


def _mix_out(x2d, mods, big, og, pool_w, pool_scale, pool_proj, dn_proj, w_out, *, batch, seq):
    steps = seq // TM_OUT
    n = batch * seq

    def row_map(b, t):
        return b * steps + t

    return pl.pallas_call(
        _mix_out_kernel,
        grid=(batch, steps),
        in_specs=[
            pl.BlockSpec((TM_OUT, D_MODEL), lambda b, t: (row_map(b, t), 0)),
            pl.BlockSpec((None, MOD_ROWS, D_MODEL), lambda b, t: (b, 0, 0)),
            pl.BlockSpec((TM_OUT, POOL_WIDTH), lambda b, t: (row_map(b, t), 12)),
            pl.BlockSpec((TM_OUT, D_MODEL), lambda b, t: (row_map(b, t), 4)),
            pl.BlockSpec((TM_OUT, D_MODEL), lambda b, t: (row_map(b, t), 5)),
            pl.BlockSpec((TM_OUT, DN_WIDTH), lambda b, t: (row_map(b, t), 0)),
            pl.BlockSpec((len(POOL_WINDOWS), POOL_GROUP_DIM, POOL_GROUP_DIM),
                         lambda b, t: (0, 0, 0)),
            pl.BlockSpec((1, POOL_WIDTH), lambda b, t: (0, 0)),
            pl.BlockSpec((POOL_WIDTH, D_MODEL), lambda b, t: (0, 0)),
            pl.BlockSpec((DN_WIDTH, D_MODEL), lambda b, t: (0, 0)),
            pl.BlockSpec((D_MODEL, D_MODEL), lambda b, t: (0, 0)),
        ],
        out_specs=pl.BlockSpec((TM_OUT, D_MODEL), lambda b, t: (row_map(b, t), 0)),
        out_shape=jax.ShapeDtypeStruct((n, D_MODEL), F32),
        scratch_shapes=[pltpu.VMEM((TM_OUT, POOL_WIDTH), BF16)],
        compiler_params=pltpu.CompilerParams(
            dimension_semantics=("arbitrary", "arbitrary"), vmem_limit_bytes=VMEM_LIMIT),
        name="mix_out",
    )(x2d, mods, big, big, big, og, pool_w, pool_scale, pool_proj, dn_proj, w_out)


def _layer(x2d, c_pad, ada_w, ada_b, norm_g, ffn1_w_in, ffn1_w_out, ffn2_w_in, ffn2_w_out,
           mix_w_in, conv_w, a_log, dt_bias, dn_norm_g, pool_w, pool_scale, pool_proj,
           dn_proj, mix_w_out, final_g, *, batch, seq, final):
    mod = _ada(c_pad, ada_w, ada_b[None, :])
    mods = mod[:batch].reshape(batch, 9, D_MODEL)
    mods = jnp.pad(mods, ((0, 0), (0, MOD_ROWS - 9), (0, 0)))

    fg = final_g[None, :]
    x2d, = _ffn(x2d, mods, norm_g[0][None, :], ffn1_w_in.astype(BF16), ffn1_w_out.astype(BF16),
                fg, sub=0, final=False, seq=seq)
    w_mix = mix_w_in.astype(BF16)

    gate_pad = LANES - GATE_COPIES * GATE_LANES
    w_tail = jnp.pad(w_mix[:, -GATE_LANES:], ((0, 0), (0, LANES - GATE_LANES)))
    n_pool = len(POOL_WINDOWS) * POOL_GROUP_DIM
    big, ba, w2_in, w2_out, w_dn, w_mo, w_pp, w_pw = _mix_in(
        x2d, mods, norm_g[1][None, :], w_mix, w_tail, batch=batch, seq=seq,
        cast=(ffn2_w_in, ffn2_w_out, dn_proj, mix_w_out, pool_proj,
              pool_w.reshape(n_pool, POOL_GROUP_DIM)))

    def alpha_row(v):
        return jnp.pad(jnp.tile(jnp.pad(v, (DN_HEADS, 0)), GATE_COPIES), (0, gate_pad))[None, :]

    alog_row = alpha_row(a_log)
    dtb_row = alpha_row(dt_bias)
    og = _deltanet(big, ba, conv_w, alog_row, dtb_row, dn_norm_g[None, :],
                   batch=batch, seq=seq)

    x2d = _mix_out(x2d, mods, big, og, w_pw.reshape(pool_w.shape), pool_scale[None, :],
                   w_pp, w_dn, w_mo, batch=batch, seq=seq)

    x2d, = _ffn(x2d, mods, norm_g[2][None, :], w2_in, w2_out, fg, sub=2, final=final, seq=seq)
    return x2d


def kernel(x, c, ada_w, ada_b, norm_g, ffn1_w_in, ffn1_w_out, ffn2_w_in, ffn2_w_out, mix_w_in, conv_w, a_log, dt_bias, dn_norm_g, pool_w, pool_scale, pool_proj, dn_proj, mix_w_out, final_g):
    batch, seq, d = x.shape
    depth = ada_w.shape[0]
    x2d = x.reshape(batch * seq, d)
    c_pad = jnp.pad(c, ((0, SUBLANES - batch), (0, 0)))
    for l in range(depth):
        x2d = _layer(x2d, c_pad, ada_w[l], ada_b[l], norm_g[l], ffn1_w_in[l], ffn1_w_out[l],
                     ffn2_w_in[l], ffn2_w_out[l], mix_w_in[l], conv_w[l], a_log[l],
                     dt_bias[l], dn_norm_g[l], pool_w[l], pool_scale[l], pool_proj[l],
                     dn_proj[l], mix_w_out[l], final_g,
                     batch=batch, seq=seq, final=(l == depth - 1))
    return x2d.reshape(batch, seq, d)
```

```python
import functools

import jax
import jax.numpy as jnp
import numpy as np
from jax import lax
from jax.experimental import pallas as pl
from jax.experimental.pallas import tpu as pltpu

F32 = jnp.float32
BF16 = jnp.bfloat16

D_MODEL = 1024
POOL_WINDOWS = (2, 4, 8, 16)
POOL_GROUP_DIM = 128
POOL_WIDTH = 512
DN_HEAD_DIM = 128
DN_HEADS = 8
DN_WIDTH = 1024
CONV_WIDTH = 4
CHUNK = 64
FFN_HIDDEN = 2816
RMS_EPS = 1e-6
L2_EPS = 1e-6

LANES = 128
SUBLANES = 8
BF16_ROWS = 16
VMEM_LIMIT = 56 * 1024 * 1024

TM_FFN = 1024
TH_FFN = 256
TM_MIX = 512
TN_MIX = 512
EPI_ROWS = 64
DN_CHUNKS_PER_STEP = 8
DN_GROUP_CHUNKS = 4
GROUP_HEADS = 4
N_GROUPS = DN_HEADS // GROUP_HEADS
PACK = GROUP_HEADS * CHUNK
PACKED = DN_HEADS * CHUNK
TM_OUT = 1024
POOL_HALO = 16
CONV_HALO = 16
MOD_ROWS = 16
GATE_LANES = 2 * DN_HEADS
GATE_COPIES = 3


def _dot(a, b):
    return jnp.dot(a, b, preferred_element_type=F32)


def _dot_nt(a, b):
    return lax.dot_general(a, b, (((1,), (1,)), ((), ())), preferred_element_type=F32)


def _dot_tn(a, b):
    return lax.dot_general(a, b, (((0,), (0,)), ((), ())), preferred_element_type=F32)


def _sigmoid(x):
    return jax.nn.sigmoid(x)


def _silu(x):
    return x * _sigmoid(x)


def _rms_norm(x, g):
    ms = jnp.mean(x * x, axis=-1, keepdims=True)
    return (x * lax.rsqrt(ms + RMS_EPS)) * g


def _modulated_norm(x, g, mod_ref, sub):
    shift = mod_ref[3 * sub + 0:3 * sub + 1, :]
    scale = mod_ref[3 * sub + 1:3 * sub + 2, :]
    return _rms_norm(x, g) * (1.0 + scale) + shift


def _const_spec(shape):
    nd = len(shape)
    return pl.BlockSpec(shape, lambda *_: (0,) * nd, pipeline_mode=pl.Buffered(1))


def _side_cast_specs(arrays, n_steps, step_of):
    in_specs, out_specs, out_shapes = [], [], []
    for a in arrays:
        rows, cols = a.shape
        n_blocks = n_steps
        while rows % n_blocks or (rows // n_blocks) % BF16_ROWS:
            n_blocks //= 2
        repeat = n_steps // n_blocks
        spec = pl.BlockSpec((rows // n_blocks, cols),
                            lambda *idx, repeat=repeat: (step_of(*idx) // repeat, 0))
        in_specs.append(spec)
        out_specs.append(spec)
        out_shapes.append(jax.ShapeDtypeStruct(a.shape, BF16))
    return in_specs, out_specs, out_shapes


def _side_cast(in_refs, out_refs):
    for src_ref, dst_ref in zip(in_refs, out_refs):
        dst_ref[...] = src_ref[...].astype(BF16)


def _ada_kernel(c_ref, w_ref, b_ref, o_ref):
    s = _silu(c_ref[...]).astype(BF16)
    o_ref[...] = _dot(s, w_ref[...].astype(BF16)) + b_ref[...]


def _ada(c_pad, ada_w, ada_b):
    n = ada_w.shape[1]
    tn = n // 4
    return pl.pallas_call(
        _ada_kernel,
        grid=(n // tn,),
        in_specs=[
            pl.BlockSpec((SUBLANES, D_MODEL), lambda j: (0, 0)),
            pl.BlockSpec((D_MODEL, tn), lambda j: (0, j)),
            pl.BlockSpec((1, tn), lambda j: (0, j)),
        ],
        out_specs=pl.BlockSpec((SUBLANES, tn), lambda j: (0, j)),
        out_shape=jax.ShapeDtypeStruct((SUBLANES, n), F32),
        compiler_params=pltpu.CompilerParams(
            dimension_semantics=("arbitrary",), vmem_limit_bytes=VMEM_LIMIT),
        name="ada",
    )(c_pad, ada_w, ada_b)


def _ffn_kernel(*refs, sub, final, n_cast):
    x_ref, mod_ref, g_ref, wi_ref, wo_ref, fg_ref = refs[:6]
    cast_in = refs[6:6 + n_cast]
    o_ref = refs[6 + n_cast]
    cast_out = refs[7 + n_cast:7 + 2 * n_cast]
    h_scr, acc_scr = refs[7 + 2 * n_cast:]
    _side_cast(cast_in, cast_out)
    x = x_ref[...]
    h_scr[...] = _modulated_norm(x, g_ref[...], mod_ref, sub).astype(BF16)
    n_chunks = FFN_HIDDEN // TH_FFN
    for j in range(n_chunks):
        cols = slice(j * TH_FFN, (j + 1) * TH_FFN)
        up_cols = slice(FFN_HIDDEN + j * TH_FFN, FFN_HIDDEN + (j + 1) * TH_FFN)
        h = h_scr[...]
        gate = _dot(h, wi_ref[:, cols])
        up = _dot(h, wi_ref[:, up_cols])
        act = (_silu(gate) * up).astype(BF16)
        part = _dot(act, wo_ref[cols, :])
        if j == 0:
            acc_scr[...] = part
        else:
            acc_scr[...] += part
    res_gate = mod_ref[3 * sub + 2:3 * sub + 3, :]
    y = x_ref[...] + (0.5 * res_gate) * acc_scr[...]
    if final:
        y = _rms_norm(y, fg_ref[...])
    o_ref[...] = y


def _ffn(x2d, mods, norm_g, w_in, w_out, final_g, *, sub, final, seq, cast=()):
    n = x2d.shape[0]
    tiles_per_seq = seq // TM_FFN
    cast_in, cast_out, cast_shapes = _side_cast_specs(cast, n // TM_FFN, lambda i: i)
    return pl.pallas_call(
        functools.partial(_ffn_kernel, sub=sub, final=final, n_cast=len(cast)),
        grid=(n // TM_FFN,),
        in_specs=[
            pl.BlockSpec((TM_FFN, D_MODEL), lambda i: (i, 0)),
            pl.BlockSpec((None, MOD_ROWS, D_MODEL), lambda i: (i // tiles_per_seq, 0, 0)),
            _const_spec((1, D_MODEL)),
            _const_spec((D_MODEL, 2 * FFN_HIDDEN)),
            _const_spec((FFN_HIDDEN, D_MODEL)),
            _const_spec((1, D_MODEL)),
        ] + cast_in,
        out_specs=[pl.BlockSpec((TM_FFN, D_MODEL), lambda i: (i, 0))] + cast_out,
        out_shape=[jax.ShapeDtypeStruct((n, D_MODEL), F32)] + cast_shapes,
        scratch_shapes=[
            pltpu.VMEM((TM_FFN, D_MODEL), BF16),
            pltpu.VMEM((TM_FFN, D_MODEL), F32),
        ],
        compiler_params=pltpu.CompilerParams(
            dimension_semantics=("arbitrary",), vmem_limit_bytes=VMEM_LIMIT),
        name="ffn%d" % sub,
    )(x2d, mods, norm_g, w_in, w_out, final_g, *cast)


def _mix_in_kernel(*refs, n_cast):
    x_ref, mod_ref, g_ref, w_ref, wtail_ref = refs[:5]
    cast_in = refs[5:5 + n_cast]
    o_ref, ba_ref = refs[5 + n_cast:7 + n_cast]
    cast_out = refs[7 + n_cast:7 + 2 * n_cast]
    h_scr, phalo = refs[7 + 2 * n_cast:]
    _mix_in_body(x_ref, mod_ref, g_ref, w_ref, wtail_ref, o_ref, ba_ref, h_scr, phalo)
    _side_cast(cast_in, cast_out)


def _mix_in_body(x_ref, mod_ref, g_ref, w_ref, wtail_ref, o_ref, ba_ref, h_scr, phalo):
    t = pl.program_id(1)

    @pl.when(t == 0)
    def _():
        phalo[...] = jnp.zeros(phalo.shape, F32)

    h_scr[...] = _modulated_norm(x_ref[...], g_ref[...], mod_ref, 1).astype(BF16)

    sub_pool = lax.broadcasted_iota(jnp.int32, (SUBLANES, POOL_GROUP_DIM), 0)

    def shift_rows(xb, s, sub):
        n = xb.shape[0] // SUBLANES
        rots = [pltpu.roll(xb[SUBLANES * k:SUBLANES * (k + 1)], s, axis=0) for k in range(n)]
        return jnp.concatenate(
            [jnp.where(sub < s, rots[k - 1], rots[k]) for k in range(1, n)], axis=0)

    def pool_block(acc, out0, gi, r0):
        win = POOL_WINDOWS[gi]
        lanes = slice(gi * POOL_GROUP_DIM, (gi + 1) * POOL_GROUP_DIM)
        gl = slice(out0 + gi * POOL_GROUP_DIM, out0 + (gi + 1) * POOL_GROUP_DIM)
        if r0 == 0:
            xb = jnp.concatenate([phalo[:, lanes], acc[0:EPI_ROWS, lanes]], axis=0)
            phalo[:, lanes] = acc[TM_MIX - POOL_HALO:, lanes]
        else:
            xb = acc[r0 - POOL_HALO:r0 + EPI_ROWS, lanes]
        x0 = xb[POOL_HALO:]
        wsum = xb[SUBLANES:]
        if win > SUBLANES:
            wsum = wsum + xb[:-SUBLANES]
        lag = 1
        while lag < min(win, SUBLANES):
            prev = jnp.concatenate([xb[:SUBLANES], wsum], axis=0)
            wsum = wsum + shift_rows(prev, lag, sub_pool)
            lag *= 2
        wsum = wsum[SUBLANES:]
        pos = (t * TM_MIX + r0 + 1
               + lax.broadcasted_iota(jnp.int32, (EPI_ROWS, 1), 0)).astype(F32)
        pooled = wsum / jnp.minimum(pos, float(win)) - x0
        o_ref[r0:r0 + EPI_ROWS, gl] = pooled.astype(BF16)

    def project(col0):
        return _dot(h_scr[...], w_ref[:, col0:col0 + TN_MIX])

    acc = project(0)
    for gi in range(len(POOL_WINDOWS)):
        for r0 in range(0, TM_MIX, EPI_ROWS):
            pool_block(acc, 6 * D_MODEL, gi, r0)

    for j in range(4 * DN_WIDTH // TN_MIX):
        cols = slice(j * TN_MIX, (j + 1) * TN_MIX)
        acc = project(POOL_WIDTH + j * TN_MIX)
        if (j * TN_MIX) // DN_WIDTH < 3:
            o_ref[:, cols] = acc.astype(BF16)
        else:
            o_ref[:, cols] = _silu(acc).astype(BF16)

    g0 = POOL_WIDTH + 4 * DN_WIDTH
    prev = project(g0)
    slab = prev[:, :LANES]
    lane = lax.broadcasted_iota(jnp.int32, slab.shape, 1)
    copies = jnp.zeros_like(slab)
    for p in reversed(range(GATE_COPIES)):
        shifted = slab if p == 0 else pltpu.roll(slab, p * GATE_LANES, axis=1)
        copies = jnp.where(lane < (p + 1) * GATE_LANES, shifted, copies)
    ba_ref[...] = copies
    for j in range(2 * D_MODEL // TN_MIX):
        if j == 2 * D_MODEL // TN_MIX - 1:
            nxt = _dot(h_scr[...], wtail_ref[...])
        else:
            nxt = project(g0 + (j + 1) * TN_MIX)
        gates = jnp.concatenate([prev[:, GATE_LANES:], nxt[:, :GATE_LANES]], axis=1)
        o_ref[:, 4 * DN_WIDTH + j * TN_MIX:4 * DN_WIDTH + (j + 1) * TN_MIX] = (
            _sigmoid(gates).astype(BF16))
        prev = nxt


def _mix_in(x2d, mods, norm_g, w_mix, w_tail, *, batch, seq, cast=()):
    n = x2d.shape[0]
    n_out = 6 * D_MODEL + POOL_WIDTH
    steps = seq // TM_MIX
    assert TN_MIX == POOL_WIDTH and w_mix.shape[1] == n_out + GATE_LANES
    cast_in, cast_out, cast_shapes = _side_cast_specs(
        cast, batch * steps, lambda b, t: b * steps + t)
    return pl.pallas_call(
        functools.partial(_mix_in_kernel, n_cast=len(cast)),
        grid=(batch, steps),
        in_specs=[
            pl.BlockSpec((TM_MIX, D_MODEL), lambda b, t: (b * steps + t, 0)),
            pl.BlockSpec((None, MOD_ROWS, D_MODEL), lambda b, t: (b, 0, 0)),
            _const_spec((1, D_MODEL)),
            _const_spec(w_mix.shape),
            _const_spec((D_MODEL, LANES)),
        ] + cast_in,
        out_specs=[
            pl.BlockSpec((TM_MIX, n_out), lambda b, t: (b * steps + t, 0)),
            pl.BlockSpec((TM_MIX, LANES), lambda b, t: (b * steps + t, 0)),
        ] + cast_out,
        out_shape=[
            jax.ShapeDtypeStruct((n, n_out), BF16),
            jax.ShapeDtypeStruct((n, LANES), F32),
        ] + cast_shapes,
        scratch_shapes=[
            pltpu.VMEM((TM_MIX, D_MODEL), BF16),
            pltpu.VMEM((POOL_HALO, POOL_WIDTH), F32),
        ],
        compiler_params=pltpu.CompilerParams(
            dimension_semantics=("arbitrary", "arbitrary"), vmem_limit_bytes=VMEM_LIMIT),
        name="mix_in",
    )(x2d, mods, norm_g, w_mix, w_tail, *cast)


def _split3(x):
    hi = x.astype(BF16)
    r = x - hi.astype(F32)
    mid = r.astype(BF16)
    lo = (r - mid.astype(F32)).astype(BF16)
    return hi, mid, lo


def _block_diag(blocks):
    n = len(blocks)
    zero = jnp.zeros_like(blocks[0])
    rows = [jnp.concatenate([blocks[i] if j == i else zero for j in range(n)], axis=1)
            for i in range(n)]
    return jnp.concatenate(rows, axis=0)


def _head_cols(h, width=DN_HEAD_DIM):
    return slice(h * width, (h + 1) * width)


def _l2n_heads(x, scale):
    outs = []
    for h in range(x.shape[1] // DN_HEAD_DIM):
        xh = x[:, _head_cols(h)]
        inv = lax.rsqrt(jnp.sum(xh * xh, axis=-1, keepdims=True) + L2_EPS)
        outs.append(xh * (inv * scale))
    return jnp.concatenate(outs, axis=1)


def _dn_constants():
    tri = np.tril(np.ones((CHUNK, CHUNK), np.float32))
    tri_blk = np.tile(np.kron(np.eye(DN_CHUNKS_PER_STEP, dtype=np.float32), tri), (1, 3))
    e64 = np.zeros((LANES, PACKED), np.float32)
    eb64 = np.zeros((LANES, PACKED), np.float32)
    e128 = np.zeros((LANES, DN_WIDTH), np.float32)
    eb128 = np.zeros((LANES, DN_WIDTH), np.float32)
    for h in range(DN_HEADS):
        e64[DN_HEADS + h, _head_cols(h, CHUNK)] = 1.0
        eb64[h, _head_cols(h, CHUNK)] = 1.0
        e128[DN_HEADS + h, _head_cols(h)] = 1.0
        eb128[h, _head_cols(h)] = 1.0
    eall = np.concatenate([eb64, e64, eb128, e128], axis=1)
    eall3 = np.zeros_like(eall)
    for p in range(GATE_COPIES):
        eall3[p * GATE_LANES:(p + 1) * GATE_LANES] = eall[:GATE_LANES]
    r = np.arange(CHUNK)[:, None]
    c = np.arange(PACKED)[None, :] % CHUNK
    u3 = np.tile((r <= c).astype(np.float32), (3, 1))
    rr = np.arange(PACK)[:, None] // CHUNK
    cc = np.arange(PACK)[None, :] // CHUNK
    bdm = (rr == cc).astype(np.float32)
    dshift = np.zeros(((CONV_WIDTH - 1) * CHUNK, CONV_HALO + CHUNK), np.float32)
    for s in range(1, CONV_WIDTH):
        dshift[(s - 1) * CHUNK + np.arange(CHUNK), CONV_HALO + np.arange(CHUNK) - s] = 1.0
    return tuple(jnp.asarray(a, BF16) for a in (tri_blk, e64, eall3, u3, bdm, dshift))


def _deltanet_kernel(q_ref, k_ref, v_ref, zg_ref, ba_ref, cw_ref, alog_ref, dtb_ref, ng_ref,
                     trib_ref, e64_ref, eall3_ref, u3_ref, bdm_ref, dshift_ref,
                     o_ref, s_scr, xhalo):
    t = pl.program_id(1)
    rows = DN_CHUNKS_PER_STEP * CHUNK

    @pl.when(t == 0)
    def _():
        s_scr[...] = jnp.zeros(s_scr.shape, F32)
        xhalo[...] = jnp.zeros(xhalo.shape, BF16)

    raw_refs = (q_ref, k_ref, v_ref)

    def conv_silu(c, sec):
        ref = raw_refs[sec]
        cols = slice(sec * DN_WIDTH, (sec + 1) * DN_WIDTH)
        if c == 0:
            xe = jnp.concatenate([xhalo[:, cols], ref[0:CHUNK, :]], axis=0)
        else:
            xe = ref[c * CHUNK - CONV_HALO:(c + 1) * CHUNK, :]
        shifted = _dot(dshift_ref[...], xe)
        y = cw_ref[CONV_WIDTH - 1:CONV_WIDTH, cols] * xe[CONV_HALO:].astype(F32)
        for s in range(1, CONV_WIDTH):
            tap = CONV_WIDTH - 1 - s
            y = y + cw_ref[tap:tap + 1, cols] * shifted[(s - 1) * CHUNK:s * CHUNK]
        return _silu(y)

    prow = lax.broadcasted_iota(jnp.int32, (CHUNK, PACKED), 0)
    pcol = lax.broadcasted_iota(jnp.int32, (CHUNK, PACKED), 1) & (CHUNK - 1)
    causal_t = prow >= pcol
    strict_t = prow > pcol
    eye_t = jnp.where(prow == pcol, 1.0, 0.0).astype(F32)
    gate_lane = lax.broadcasted_iota(jnp.int32, (rows, LANES), 1)
    is_beta_lane = (gate_lane & (GATE_LANES - 1)) < DN_HEADS

    neg_decay_rate = -jnp.exp(alog_ref[...])
    dt_bias = dtb_ref[...]
    norm_g = ng_ref[...]
    bdm = bdm_ref[...]

    def group_cols(g, width):
        return slice(g * GROUP_HEADS * width, (g + 1) * GROUP_HEADS * width)

    def head_blocks(x, g):
        return [x[:, _head_cols(g * GROUP_HEADS + i)] for i in range(GROUP_HEADS)]

    def packed_block_diag(wb):
        return jnp.concatenate([wb] * GROUP_HEADS, axis=0) * bdm

    ba = ba_ref[...]
    beta = _sigmoid(ba)
    xa = ba + dt_bias
    softplus = jnp.maximum(xa, 0.0) + jnp.log1p(jnp.exp(-jnp.abs(xa)))
    g_log = neg_decay_rate * softplus

    gstack = jnp.concatenate(_split3(g_log), axis=0)
    gc = _dot(trib_ref[...], gstack)
    geb = _dot(gstack, e64_ref[...]).astype(BF16)
    bgc = jnp.where(is_beta_lane, beta, gc)
    hi = bgc.astype(BF16).astype(F32)
    rem = bgc - hi
    mid = rem.astype(BF16).astype(F32)
    piece = jnp.where(gate_lane < GATE_LANES, hi,
                      jnp.where(gate_lane < 2 * GATE_LANES, mid, rem - mid))
    x = _dot(piece.astype(BF16), eall3_ref[...])
    beta64 = x[:, :PACKED]
    gcol64 = x[:, PACKED:2 * PACKED]
    beta128 = x[:, 2 * PACKED:2 * PACKED + DN_WIDTH]
    gc128 = x[:, 2 * PACKED + DN_WIDTH:]
    ones_lhs = jnp.ones((2 * SUBLANES, 3 * CHUNK), BF16)

    st = {}

    def prepare(cs):
        for c in cs:
            rs = slice(c * CHUNK, (c + 1) * CHUNK)
            qn = _l2n_heads(conv_silu(c, 0), DN_HEAD_DIM ** -0.5)
            kn = _l2n_heads(conv_silu(c, 1), 1.0)
            pieces = [geb[p * rows + c * CHUNK:p * rows + (c + 1) * CHUNK] for p in range(3)]
            grow = _dot(ones_lhs, jnp.concatenate(pieces, axis=0) * u3_ref[...])[0:1]
            st[c] = dict(qb=qn.astype(BF16), kb=kn.astype(BF16), qn=qn, kn=kn,
                         vc=conv_silu(c, 2),
                         gdiff=gcol64[rs] - grow,
                         beta64=beta64[rs], beta128=beta128[rs], gc128=gc128[rs])
            yield
        sts = [st[c] for c in cs]

        for s in sts:
            qk, kk = [], []
            for g in range(N_GROUPS):
                gs = group_cols(g, DN_HEAD_DIM)
                bk = _block_diag(head_blocks(s["kb"], g))
                r = _dot_nt(jnp.concatenate([s["qb"][:, gs], s["kb"][:, gs]], axis=0), bk)
                qk.append(r[:CHUNK])
                kk.append(r[CHUNK:])
            s["qk"] = jnp.concatenate(qk, axis=1)
            s["kk"] = jnp.concatenate(kk, axis=1)
            s.pop("qb")
            s.pop("kb")
        yield

        for s in sts:
            decay = jnp.exp(jnp.where(causal_t, s.pop("gdiff"), -jnp.inf))
            a_mat = jnp.where(strict_t, s.pop("beta64") * s.pop("kk") * decay, 0.0)
            qkd = s.pop("qk") * decay
            s["qkd"] = [qkd[:, group_cols(g, CHUNK)].astype(BF16) for g in range(N_GROUPS)]
            s["w"] = [-a_mat[:, group_cols(g, CHUNK)] for g in range(N_GROUPS)]
            p0 = eye_t - a_mat
            s["p"] = [p0[:, group_cols(g, CHUNK)] for g in range(N_GROUPS)]
        yield

        for s in sts:
            for g in range(N_GROUPS):
                wb = s["w"][g].astype(BF16)
                s["w"][g] = _dot(wb, packed_block_diag(wb))
        yield
        n = 4
        while n < CHUNK:
            for s in sts:
                for g in range(N_GROUPS):
                    wb = s["w"][g].astype(BF16)
                    r = _dot(jnp.concatenate([wb, s["p"][g].astype(BF16)], axis=0),
                             packed_block_diag(wb))
                    s["w"][g] = r[:CHUNK]
                    s["p"][g] = s["p"][g] + r[CHUNK:]
            yield
            n *= 2
        for s in sts:
            for g in range(N_GROUPS):
                s["p"][g] = s["p"][g] + _dot(s["p"][g].astype(BF16),
                                             packed_block_diag(s["w"][g].astype(BF16)))
            s.pop("w")
        yield

        for s in sts:
            gc128_c = s.pop("gc128")
            beta128_c = s.pop("beta128")
            eg = jnp.exp(gc128_c)
            rv = (beta128_c * s.pop("vc")).astype(BF16)
            rk = ((beta128_c * eg) * s["kn"]).astype(BF16)
            us, ws = [], []
            for g in range(N_GROUPS):
                rhs = jnp.concatenate([_block_diag(head_blocks(rv, g)),
                                       _block_diag(head_blocks(rk, g))], axis=1)
                sol = _dot(s["p"][g].astype(BF16), rhs)
                us.append(sol[:, :GROUP_HEADS * DN_HEAD_DIM])
                ws.append(sol[:, GROUP_HEADS * DN_HEAD_DIM:])
            s.pop("p")
            s["u"] = jnp.concatenate(us, axis=1)
            w_all = jnp.concatenate(ws, axis=1)
            g_last = gc128_c[CHUNK - 1:CHUNK, :]
            q_dec = s.pop("qn") * eg
            s["wq"] = jnp.concatenate([w_all, q_dec], axis=0).astype(BF16)
            s["kdec"] = (s.pop("kn") * jnp.exp(g_last - gc128_c)).astype(BF16)
            s["sdecay"] = jnp.exp(g_last)
            yield

    state = [s_scr[h] for h in range(DN_HEADS)]
    pair = 2 * DN_HEAD_DIM

    def recurrence(cs):
        for c in cs:
            s = st.pop(c)
            r0 = c * CHUNK
            ws = []
            for p in range(DN_HEADS // 2):
                rhs = _block_diag([state[2 * p].astype(BF16), state[2 * p + 1].astype(BF16)])
                ws.append(_dot(s["wq"][:, p * pair:(p + 1) * pair], rhs))
            yield
            ws = jnp.concatenate(ws, axis=1)
            v_new = (s["u"] - ws[:CHUNK]).astype(BF16)
            o = []
            for g in range(N_GROUPS):
                o.append(ws[CHUNK:, group_cols(g, DN_HEAD_DIM)]
                         + _dot(s["qkd"][g], _block_diag(head_blocks(v_new, g))))
            upds = []
            for p in range(DN_HEADS // 2):
                ps = slice(p * pair, (p + 1) * pair)
                upds.append(_dot_tn(s["kdec"][:, ps], v_new[:, ps]))
            yield
            for p in range(DN_HEADS // 2):
                for i in range(2):
                    h = 2 * p + i
                    blk = slice(i * DN_HEAD_DIM, (i + 1) * DN_HEAD_DIM)
                    state[h] = state[h] * s["sdecay"][:, _head_cols(h)] + upds[p][blk, blk]
            o = jnp.concatenate(o, axis=1)
            for h in range(DN_HEADS):
                hs = _head_cols(h)
                zg = zg_ref[r0:r0 + CHUNK, hs].astype(F32)
                o_ref[r0:r0 + CHUNK, hs] = (_rms_norm(o[:, hs], norm_g) * zg).astype(BF16)

    def interleave(streams, shares):
        live = [True] * len(streams)
        while any(live):
            for i, stream in enumerate(streams):
                for _ in range(shares[i]):
                    if live[i]:
                        try:
                            next(stream)
                        except StopIteration:
                            live[i] = False

    groups = [list(range(i, i + DN_GROUP_CHUNKS))
              for i in range(0, DN_CHUNKS_PER_STEP, DN_GROUP_CHUNKS)]
    interleave([prepare(groups[0])], [1])
    for prev_group, group in zip(groups[:-1], groups[1:]):
        interleave([recurrence(prev_group), prepare(group)], [1, 3])
    interleave([recurrence(groups[-1])], [1])

    for h in range(DN_HEADS):
        s_scr[h] = state[h]
    for sec, ref in enumerate(raw_refs):
        xhalo[:, sec * DN_WIDTH:(sec + 1) * DN_WIDTH] = ref[rows - CONV_HALO:rows, :]


def _deltanet(big, ba, conv_w, alog_row, dtb_row, dn_norm_g, *, batch, seq):
    rows = DN_CHUNKS_PER_STEP * CHUNK
    steps = seq // rows
    n = batch * seq
    consts = _dn_constants()

    def tok_spec(col_block):
        return pl.BlockSpec((rows, DN_WIDTH), lambda b, t: (b * steps + t, col_block))

    def full_spec(a):
        return pl.BlockSpec(a.shape, lambda b, t: (0,) * a.ndim)

    return pl.pallas_call(
        _deltanet_kernel,
        grid=(batch, steps),
        in_specs=[
            tok_spec(0), tok_spec(1), tok_spec(2), tok_spec(3),
            pl.BlockSpec((rows, LANES), lambda b, t: (b * steps + t, 0)),
            full_spec(conv_w), full_spec(alog_row), full_spec(dtb_row), full_spec(dn_norm_g),
        ] + [full_spec(a) for a in consts],
        out_specs=pl.BlockSpec((rows, DN_WIDTH), lambda b, t: (b * steps + t, 0)),
        out_shape=jax.ShapeDtypeStruct((n, DN_WIDTH), BF16),
        scratch_shapes=[
            pltpu.VMEM((DN_HEADS, DN_HEAD_DIM, DN_HEAD_DIM), F32),
            pltpu.VMEM((CONV_HALO, 3 * DN_WIDTH), BF16),
        ],
        compiler_params=pltpu.CompilerParams(
            dimension_semantics=("arbitrary", "arbitrary"), vmem_limit_bytes=VMEM_LIMIT),
        name="deltanet",
    )(big, big, big, big, ba, conv_w, alog_row, dtb_row, dn_norm_g, *consts)


def _mix_out_kernel(x_ref, mod_ref, pooled_ref, gp_ref, gd_ref, og_ref, pw_ref, ps_ref,
                    pp_ref, dp_ref, wo_ref, o_ref, ya_scr):
    for gi in range(len(POOL_WINDOWS)):
        cols = slice(gi * POOL_GROUP_DIM, (gi + 1) * POOL_GROUP_DIM)
        ya_g = _dot(pooled_ref[:, cols], pw_ref[gi]) * ps_ref[:, cols]
        ya_scr[:, cols] = ya_g.astype(BF16)

    ya = _dot(ya_scr[...], pp_ref[...])
    yb = _dot(og_ref[...], dp_ref[...])
    merged = gp_ref[...].astype(F32) * ya + gd_ref[...].astype(F32) * yb
    out = _dot(merged.astype(BF16), wo_ref[...])
    res_gate = mod_ref[5:6, :]
    o_ref[...] = x_ref[...] + res_gate * out


def _mix_out(x2d, mods, big, og, pool_w, pool_scale, pool_proj, dn_proj, w_out, *, batch, seq):
    steps = seq // TM_OUT
    n = batch * seq

    def row_map(b, t):
        return b * steps + t

    return pl.pallas_call(
        _mix_out_kernel,
        grid=(batch, steps),
        in_specs=[
            pl.BlockSpec((TM_OUT, D_MODEL), lambda b, t: (row_map(b, t), 0)),
            pl.BlockSpec((None, MOD_ROWS, D_MODEL), lambda b, t: (b, 0, 0)),
            pl.BlockSpec((TM_OUT, POOL_WIDTH), lambda b, t: (row_map(b, t), 12)),
            pl.BlockSpec((TM_OUT, D_MODEL), lambda b, t: (row_map(b, t), 4)),
            pl.BlockSpec((TM_OUT, D_MODEL), lambda b, t: (row_map(b, t), 5)),
            pl.BlockSpec((TM_OUT, DN_WIDTH), lambda b, t: (row_map(b, t), 0)),
            pl.BlockSpec((len(POOL_WINDOWS), POOL_GROUP_DIM, POOL_GROUP_DIM),
                         lambda b, t: (0, 0, 0)),
            pl.BlockSpec((1, POOL_WIDTH), lambda b, t: (0, 0)),
            pl.BlockSpec((POOL_WIDTH, D_MODEL), lambda b, t: (0, 0)),
            pl.BlockSpec((DN_WIDTH, D_MODEL), lambda b, t: (0, 0)),
            pl.BlockSpec((D_MODEL, D_MODEL), lambda b, t: (0, 0)),
        ],
        out_specs=pl.BlockSpec((TM_OUT, D_MODEL), lambda b, t: (row_map(b, t), 0)),
        out_shape=jax.ShapeDtypeStruct((n, D_MODEL), F32),
        scratch_shapes=[pltpu.VMEM((TM_OUT, POOL_WIDTH), BF16)],
        compiler_params=pltpu.CompilerParams(
            dimension_semantics=("arbitrary", "arbitrary"), vmem_limit_bytes=VMEM_LIMIT),
        name="mix_out",
    )(x2d, mods, big, big, big, og, pool_w, pool_scale, pool_proj, dn_proj, w_out)


def _layer(x2d, c_pad, ada_w, ada_b, norm_g, ffn1_w_in, ffn1_w_out, ffn2_w_in, ffn2_w_out,
           mix_w_in, conv_w, a_log, dt_bias, dn_norm_g, pool_w, pool_scale, pool_proj,
           dn_proj, mix_w_out, final_g, *, batch, seq, final):
    mod = _ada(c_pad, ada_w, ada_b[None, :])
    mods = mod[:batch].reshape(batch, 9, D_MODEL)
    mods = jnp.pad(mods, ((0, 0), (0, MOD_ROWS - 9), (0, 0)))

    fg = final_g[None, :]
    x2d, w_mix = _ffn(x2d, mods, norm_g[0][None, :], ffn1_w_in.astype(BF16),
                      ffn1_w_out.astype(BF16), fg, sub=0, final=False, seq=seq,
                      cast=(mix_w_in,))

    gate_pad = LANES - GATE_COPIES * GATE_LANES
    w_tail = jnp.pad(w_mix[:, -GATE_LANES:], ((0, 0), (0, LANES - GATE_LANES)))
    n_pool = len(POOL_WINDOWS) * POOL_GROUP_DIM
    big, ba, w2_in, w2_out, w_dn, w_mo, w_pp, w_pw = _mix_in(
        x2d, mods, norm_g[1][None, :], w_mix, w_tail, batch=batch, seq=seq,
        cast=(ffn2_w_in, ffn2_w_out, dn_proj, mix_w_out, pool_proj,
              pool_w.reshape(n_pool, POOL_GROUP_DIM)))

    def alpha_row(v):
        return jnp.pad(jnp.tile(jnp.pad(v, (DN_HEADS, 0)), GATE_COPIES), (0, gate_pad))[None, :]

    alog_row = alpha_row(a_log)
    dtb_row = alpha_row(dt_bias)
    og = _deltanet(big, ba, conv_w, alog_row, dtb_row, dn_norm_g[None, :],
                   batch=batch, seq=seq)

    x2d = _mix_out(x2d, mods, big, og, w_pw.reshape(pool_w.shape), pool_scale[None, :],
                   w_pp, w_dn, w_mo, batch=batch, seq=seq)

    x2d, = _ffn(x2d, mods, norm_g[2][None, :], w2_in, w2_out, fg, sub=2, final=final, seq=seq)
    return x2d


def kernel(x, c, ada_w, ada_b, norm_g, ffn1_w_in, ffn1_w_out, ffn2_w_in, ffn2_w_out, mix_w_in, conv_w, a_log, dt_bias, dn_norm_g, pool_w, pool_scale, pool_proj, dn_proj, mix_w_out, final_g):
    batch, seq, d = x.shape
    depth = ada_w.shape[0]
    x2d = x.reshape(batch * seq, d)
    c_pad = jnp.pad(c, ((0, SUBLANES - batch), (0, 0)))
    for l in range(depth):
        x2d = _layer(x2d, c_pad, ada_w[l], ada_b[l], norm_g[l], ffn1_w_in[l], ffn1_w_out[l],
                     ffn2_w_in[l], ffn2_w_out[l], mix_w_in[l], conv_w[l], a_log[l],
                     dt_bias[l], dn_norm_g[l], pool_w[l], pool_scale[l], pool_proj[l],
                     dn_proj[l], mix_w_out[l], final_g,
                     batch=batch, seq=seq, final=(l == depth - 1))
    return x2d.reshape(batch, seq, d)
```

```python
import functools

import jax
import jax.numpy as jnp
import numpy as np
from jax import lax
from jax.experimental import pallas as pl
from jax.experimental.pallas import tpu as pltpu

F32 = jnp.float32
BF16 = jnp.bfloat16

D_MODEL = 1024
POOL_WINDOWS = (2, 4, 8, 16)
POOL_GROUP_DIM = 128
POOL_WIDTH = 512
DN_HEAD_DIM = 128
DN_HEADS = 8
DN_WIDTH = 1024
CONV_WIDTH = 4
CHUNK = 64
FFN_HIDDEN = 2816
RMS_EPS = 1e-6
L2_EPS = 1e-6

LANES = 128
SUBLANES = 8
BF16_ROWS = 16
VMEM_LIMIT = 56 * 1024 * 1024

TM_FFN = 1024
TH_FFN = 256
TM_MIX = 512
TN_MIX = 512
EPI_ROWS = 64
DN_CHUNKS_PER_STEP = 8
DN_GROUP_CHUNKS = 4
GROUP_HEADS = 4
N_GROUPS = DN_HEADS // GROUP_HEADS
PACK = GROUP_HEADS * CHUNK
PACKED = DN_HEADS * CHUNK
TM_OUT = 1024
POOL_HALO = 16
CONV_HALO = 16
MOD_ROWS = 16
OUT_GATES = 4 * DN_WIDTH
OUT_POOLED = OUT_GATES + 2 * D_MODEL
GATE_LANES = 2 * DN_HEADS
GATE_COPIES = 3


def _dot(a, b):
    return jnp.dot(a, b, preferred_element_type=F32)


def _dot_nt(a, b):
    return lax.dot_general(a, b, (((1,), (1,)), ((), ())), preferred_element_type=F32)


def _dot_tn(a, b):
    return lax.dot_general(a, b, (((0,), (0,)), ((), ())), preferred_element_type=F32)


def _sigmoid(x):
    return jax.nn.sigmoid(x)


def _silu(x):
    return x * _sigmoid(x)


def _rms_norm(x, g):
    ms = jnp.mean(x * x, axis=-1, keepdims=True)
    return (x * lax.rsqrt(ms + RMS_EPS)) * g


def _modulated_norm(x, g, mod_ref, sub):
    shift = mod_ref[3 * sub + 0:3 * sub + 1, :]
    scale = mod_ref[3 * sub + 1:3 * sub + 2, :]
    return _rms_norm(x, g) * (1.0 + scale) + shift


def _const_spec(shape):
    nd = len(shape)
    return pl.BlockSpec(shape, lambda *_: (0,) * nd, pipeline_mode=pl.Buffered(1))


def _side_cast_specs(arrays, n_steps, step_of):
    in_specs, out_specs, out_shapes = [], [], []
    for a in arrays:
        rows, cols = a.shape
        n_blocks = n_steps
        while rows % n_blocks or (rows // n_blocks) % BF16_ROWS:
            n_blocks //= 2
        repeat = n_steps // n_blocks
        spec = pl.BlockSpec((rows // n_blocks, cols),
                            lambda *idx, repeat=repeat: (step_of(*idx) // repeat, 0))
        in_specs.append(spec)
        out_specs.append(spec)
        out_shapes.append(jax.ShapeDtypeStruct(a.shape, BF16))
    return in_specs, out_specs, out_shapes


def _side_cast(in_refs, out_refs):
    for src_ref, dst_ref in zip(in_refs, out_refs):
        dst_ref[...] = src_ref[...].astype(BF16)


def _ada_kernel(c_ref, w_ref, b_ref, o_ref):
    s = _silu(c_ref[...]).astype(BF16)
    o_ref[...] = _dot(s, w_ref[...].astype(BF16)) + b_ref[...]


def _ada(c_pad, ada_w, ada_b):
    n = ada_w.shape[1]
    tn = n // 4
    return pl.pallas_call(
        _ada_kernel,
        grid=(n // tn,),
        in_specs=[
            pl.BlockSpec((SUBLANES, D_MODEL), lambda j: (0, 0)),
            pl.BlockSpec((D_MODEL, tn), lambda j: (0, j)),
            pl.BlockSpec((1, tn), lambda j: (0, j)),
        ],
        out_specs=pl.BlockSpec((SUBLANES, tn), lambda j: (0, j)),
        out_shape=jax.ShapeDtypeStruct((SUBLANES, n), F32),
        compiler_params=pltpu.CompilerParams(
            dimension_semantics=("arbitrary",), vmem_limit_bytes=VMEM_LIMIT),
        name="ada",
    )(c_pad, ada_w, ada_b)


def _ffn_kernel(*refs, sub, final, n_cast):
    x_ref, mod_ref, g_ref, wi_ref, wo_ref, fg_ref = refs[:6]
    cast_in = refs[6:6 + n_cast]
    o_ref = refs[6 + n_cast]
    cast_out = refs[7 + n_cast:7 + 2 * n_cast]
    h_scr, acc_scr = refs[7 + 2 * n_cast:]
    _side_cast(cast_in, cast_out)
    x = x_ref[...]
    h_scr[...] = _modulated_norm(x, g_ref[...], mod_ref, sub).astype(BF16)
    n_chunks = FFN_HIDDEN // TH_FFN
    for j in range(n_chunks):
        cols = slice(j * TH_FFN, (j + 1) * TH_FFN)
        up_cols = slice(FFN_HIDDEN + j * TH_FFN, FFN_HIDDEN + (j + 1) * TH_FFN)
        h = h_scr[...]
        gate = _dot(h, wi_ref[:, cols])
        up = _dot(h, wi_ref[:, up_cols])
        act = (_silu(gate) * up).astype(BF16)
        part = _dot(act, wo_ref[cols, :])
        if j == 0:
            acc_scr[...] = part
        else:
            acc_scr[...] += part
    res_gate = mod_ref[3 * sub + 2:3 * sub + 3, :]
    y = x_ref[...] + (0.5 * res_gate) * acc_scr[...]
    if final:
        y = _rms_norm(y, fg_ref[...])
    o_ref[...] = y


def _ffn(x2d, mods, norm_g, w_in, w_out, final_g, *, sub, final, seq, cast=()):
    n = x2d.shape[0]
    tiles_per_seq = seq // TM_FFN
    cast_in, cast_out, cast_shapes = _side_cast_specs(cast, n // TM_FFN, lambda i: i)
    return pl.pallas_call(
        functools.partial(_ffn_kernel, sub=sub, final=final, n_cast=len(cast)),
        grid=(n // TM_FFN,),
        in_specs=[
            pl.BlockSpec((TM_FFN, D_MODEL), lambda i: (i, 0)),
            pl.BlockSpec((None, MOD_ROWS, D_MODEL), lambda i: (i // tiles_per_seq, 0, 0)),
            _const_spec((1, D_MODEL)),
            _const_spec((D_MODEL, 2 * FFN_HIDDEN)),
            _const_spec((FFN_HIDDEN, D_MODEL)),
            _const_spec((1, D_MODEL)),
        ] + cast_in,
        out_specs=[pl.BlockSpec((TM_FFN, D_MODEL), lambda i: (i, 0))] + cast_out,
        out_shape=[jax.ShapeDtypeStruct((n, D_MODEL), F32)] + cast_shapes,
        scratch_shapes=[
            pltpu.VMEM((TM_FFN, D_MODEL), BF16),
            pltpu.VMEM((TM_FFN, D_MODEL), F32),
        ],
        compiler_params=pltpu.CompilerParams(
            dimension_semantics=("arbitrary",), vmem_limit_bytes=VMEM_LIMIT),
        name="ffn%d" % sub,
    )(x2d, mods, norm_g, w_in, w_out, final_g, *cast)


def _mix_in_kernel(*refs, n_cast):
    x_ref, mod_ref, g_ref, w_ref, wtail_ref = refs[:5]
    cast_in = refs[5:5 + n_cast]
    o_ref, ba_ref = refs[5 + n_cast:7 + n_cast]
    cast_out = refs[7 + n_cast:7 + 2 * n_cast]
    h_scr, phalo = refs[7 + 2 * n_cast:]
    _mix_in_body(x_ref, mod_ref, g_ref, w_ref, wtail_ref, o_ref, ba_ref, h_scr, phalo)
    _side_cast(cast_in, cast_out)


def _mix_in_body(x_ref, mod_ref, g_ref, w_ref, wtail_ref, o_ref, ba_ref, h_scr, phalo):
    t = pl.program_id(1)

    @pl.when(t == 0)
    def _():
        phalo[...] = jnp.zeros(phalo.shape, F32)

    h_scr[...] = _modulated_norm(x_ref[...], g_ref[...], mod_ref, 1).astype(BF16)

    sub_pool = lax.broadcasted_iota(jnp.int32, (SUBLANES, POOL_GROUP_DIM), 0)

    def shift_rows(xb, s, sub):
        n = xb.shape[0] // SUBLANES
        rots = [pltpu.roll(xb[SUBLANES * k:SUBLANES * (k + 1)], s, axis=0) for k in range(n)]
        return jnp.concatenate(
            [jnp.where(sub < s, rots[k - 1], rots[k]) for k in range(1, n)], axis=0)

    def pool_block(acc, out0, gi, r0):
        win = POOL_WINDOWS[gi]
        lanes = slice(gi * POOL_GROUP_DIM, (gi + 1) * POOL_GROUP_DIM)
        gl = slice(out0 + gi * POOL_GROUP_DIM, out0 + (gi + 1) * POOL_GROUP_DIM)
        if r0 == 0:
            xb = jnp.concatenate([phalo[:, lanes], acc[0:EPI_ROWS, lanes]], axis=0)
            phalo[:, lanes] = acc[TM_MIX - POOL_HALO:, lanes]
        else:
            xb = acc[r0 - POOL_HALO:r0 + EPI_ROWS, lanes]
        x0 = xb[POOL_HALO:]
        wsum = xb[SUBLANES:]
        if win > SUBLANES:
            wsum = wsum + xb[:-SUBLANES]
        lag = 1
        while lag < min(win, SUBLANES):
            prev = jnp.concatenate([xb[:SUBLANES], wsum], axis=0)
            wsum = wsum + shift_rows(prev, lag, sub_pool)
            lag *= 2
        wsum = wsum[SUBLANES:]
        pos = (t * TM_MIX + r0 + 1
               + lax.broadcasted_iota(jnp.int32, (EPI_ROWS, 1), 0)).astype(F32)
        pooled = wsum / jnp.minimum(pos, float(win)) - x0
        o_ref[r0:r0 + EPI_ROWS, gl] = pooled.astype(BF16)

    def project(col0):
        return _dot(h_scr[...], w_ref[:, col0:col0 + TN_MIX])

    acc = project(0)
    for gi in range(len(POOL_WINDOWS)):
        for r0 in range(0, TM_MIX, EPI_ROWS):
            pool_block(acc, OUT_POOLED, gi, r0)

    for j in range(4 * DN_WIDTH // TN_MIX):
        cols = slice(j * TN_MIX, (j + 1) * TN_MIX)
        acc = project(POOL_WIDTH + j * TN_MIX)
        if (j * TN_MIX) // DN_WIDTH < 3:
            o_ref[:, cols] = acc.astype(BF16)
        else:
            o_ref[:, cols] = _silu(acc).astype(BF16)

    g0 = POOL_WIDTH + 4 * DN_WIDTH
    prev = project(g0)
    slab = prev[:, :LANES]
    lane = lax.broadcasted_iota(jnp.int32, slab.shape, 1)
    copies = jnp.zeros_like(slab)
    for p in reversed(range(GATE_COPIES)):
        shifted = slab if p == 0 else pltpu.roll(slab, p * GATE_LANES, axis=1)
        copies = jnp.where(lane < (p + 1) * GATE_LANES, shifted, copies)
    ba_ref[...] = copies
    for j in range(2 * D_MODEL // TN_MIX):
        if j == 2 * D_MODEL // TN_MIX - 1:
            nxt = _dot(h_scr[...], wtail_ref[...])
        else:
            nxt = project(g0 + (j + 1) * TN_MIX)
        gates = jnp.concatenate([prev[:, GATE_LANES:], nxt[:, :GATE_LANES]], axis=1)
        o_ref[:, OUT_GATES + j * TN_MIX:OUT_GATES + (j + 1) * TN_MIX] = (
            _sigmoid(gates).astype(BF16))
        prev = nxt


def _mix_in(x2d, mods, norm_g, w_mix, w_tail, *, batch, seq, cast=()):
    n = x2d.shape[0]
    n_out = OUT_POOLED + POOL_WIDTH
    steps = seq // TM_MIX
    assert TN_MIX == POOL_WIDTH and w_mix.shape[1] == n_out + GATE_LANES
    cast_in, cast_out, cast_shapes = _side_cast_specs(
        cast, batch * steps, lambda b, t: b * steps + t)
    return pl.pallas_call(
        functools.partial(_mix_in_kernel, n_cast=len(cast)),
        grid=(batch, steps),
        in_specs=[
            pl.BlockSpec((TM_MIX, D_MODEL), lambda b, t: (b * steps + t, 0)),
            pl.BlockSpec((None, MOD_ROWS, D_MODEL), lambda b, t: (b, 0, 0)),
            _const_spec((1, D_MODEL)),
            _const_spec(w_mix.shape),
            _const_spec((D_MODEL, LANES)),
        ] + cast_in,
        out_specs=[
            pl.BlockSpec((TM_MIX, n_out), lambda b, t: (b * steps + t, 0)),
            pl.BlockSpec((TM_MIX, LANES), lambda b, t: (b * steps + t, 0)),
        ] + cast_out,
        out_shape=[
            jax.ShapeDtypeStruct((n, n_out), BF16),
            jax.ShapeDtypeStruct((n, LANES), F32),
        ] + cast_shapes,
        scratch_shapes=[
            pltpu.VMEM((TM_MIX, D_MODEL), BF16),
            pltpu.VMEM((POOL_HALO, POOL_WIDTH), F32),
        ],
        compiler_params=pltpu.CompilerParams(
            dimension_semantics=("arbitrary", "arbitrary"), vmem_limit_bytes=VMEM_LIMIT),
        name="mix_in",
    )(x2d, mods, norm_g, w_mix, w_tail, *cast)


def _split3(x):
    hi = x.astype(BF16)
    r = x - hi.astype(F32)
    mid = r.astype(BF16)
    lo = (r - mid.astype(F32)).astype(BF16)
    return hi, mid, lo


def _block_diag(blocks):
    n = len(blocks)
    zero = jnp.zeros_like(blocks[0])
    rows = [jnp.concatenate([blocks[i] if j == i else zero for j in range(n)], axis=1)
            for i in range(n)]
    return jnp.concatenate(rows, axis=0)


def _head_cols(h, width=DN_HEAD_DIM):
    return slice(h * width, (h + 1) * width)


def _l2n_heads(x, scale):
    outs = []
    for h in range(x.shape[1] // DN_HEAD_DIM):
        xh = x[:, _head_cols(h)]
        inv = lax.rsqrt(jnp.sum(xh * xh, axis=-1, keepdims=True) + L2_EPS)
        outs.append(xh * (inv * scale))
    return jnp.concatenate(outs, axis=1)


def _dn_constants():
    tri = np.tril(np.ones((CHUNK, CHUNK), np.float32))
    tri_blk = np.tile(np.kron(np.eye(DN_CHUNKS_PER_STEP, dtype=np.float32), tri), (1, 3))
    e64 = np.zeros((LANES, PACKED), np.float32)
    eb64 = np.zeros((LANES, PACKED), np.float32)
    e128 = np.zeros((LANES, DN_WIDTH), np.float32)
    eb128 = np.zeros((LANES, DN_WIDTH), np.float32)
    for h in range(DN_HEADS):
        e64[DN_HEADS + h, _head_cols(h, CHUNK)] = 1.0
        eb64[h, _head_cols(h, CHUNK)] = 1.0
        e128[DN_HEADS + h, _head_cols(h)] = 1.0
        eb128[h, _head_cols(h)] = 1.0
    eall = np.concatenate([eb64, e64, eb128, e128], axis=1)
    eall3 = np.zeros_like(eall)
    for p in range(GATE_COPIES):
        eall3[p * GATE_LANES:(p + 1) * GATE_LANES] = eall[:GATE_LANES]
    r = np.arange(CHUNK)[:, None]
    c = np.arange(PACKED)[None, :] % CHUNK
    u3 = np.tile((r <= c).astype(np.float32), (3, 1))
    rr = np.arange(PACK)[:, None] // CHUNK
    cc = np.arange(PACK)[None, :] // CHUNK
    bdm = (rr == cc).astype(np.float32)
    dshift = np.zeros(((CONV_WIDTH - 1) * CHUNK, CONV_HALO + CHUNK), np.float32)
    for s in range(1, CONV_WIDTH):
        dshift[(s - 1) * CHUNK + np.arange(CHUNK), CONV_HALO + np.arange(CHUNK) - s] = 1.0
    return tuple(jnp.asarray(a, BF16) for a in (tri_blk, e64, eall3, u3, bdm, dshift))


def _deltanet_kernel(q_ref, k_ref, v_ref, zg_ref, ba_ref, cw_ref, alog_ref, dtb_ref, ng_ref,
                     trib_ref, e64_ref, eall3_ref, u3_ref, bdm_ref, dshift_ref,
                     o_ref, s_scr, xhalo):
    t = pl.program_id(1)
    rows = DN_CHUNKS_PER_STEP * CHUNK

    @pl.when(t == 0)
    def _():
        s_scr[...] = jnp.zeros(s_scr.shape, F32)
        xhalo[...] = jnp.zeros(xhalo.shape, BF16)

    raw_refs = (q_ref, k_ref, v_ref)

    def conv_silu(c, sec):
        ref = raw_refs[sec]
        cols = slice(sec * DN_WIDTH, (sec + 1) * DN_WIDTH)
        if c == 0:
            xe = jnp.concatenate([xhalo[:, cols], ref[0:CHUNK, :]], axis=0)
        else:
            xe = ref[c * CHUNK - CONV_HALO:(c + 1) * CHUNK, :]
        shifted = _dot(dshift_ref[...], xe)
        y = cw_ref[CONV_WIDTH - 1:CONV_WIDTH, cols] * xe[CONV_HALO:].astype(F32)
        for s in range(1, CONV_WIDTH):
            tap = CONV_WIDTH - 1 - s
            y = y + cw_ref[tap:tap + 1, cols] * shifted[(s - 1) * CHUNK:s * CHUNK]
        return _silu(y)

    prow = lax.broadcasted_iota(jnp.int32, (CHUNK, PACKED), 0)
    pcol = lax.broadcasted_iota(jnp.int32, (CHUNK, PACKED), 1) & (CHUNK - 1)
    causal_t = prow >= pcol
    strict_t = prow > pcol
    eye_t = jnp.where(prow == pcol, 1.0, 0.0).astype(F32)
    gate_lane = lax.broadcasted_iota(jnp.int32, (rows, LANES), 1)
    is_beta_lane = (gate_lane & (GATE_LANES - 1)) < DN_HEADS

    neg_decay_rate = -jnp.exp(alog_ref[...])
    dt_bias = dtb_ref[...]
    norm_g = ng_ref[...]
    bdm = bdm_ref[...]

    def group_cols(g, width):
        return slice(g * GROUP_HEADS * width, (g + 1) * GROUP_HEADS * width)

    def head_blocks(x, g):
        return [x[:, _head_cols(g * GROUP_HEADS + i)] for i in range(GROUP_HEADS)]

    def packed_block_diag(wb):
        return jnp.concatenate([wb] * GROUP_HEADS, axis=0) * bdm

    ba = ba_ref[...]
    beta = _sigmoid(ba)
    xa = ba + dt_bias
    softplus = jnp.maximum(xa, 0.0) + jnp.log1p(jnp.exp(-jnp.abs(xa)))
    g_log = neg_decay_rate * softplus

    gstack = jnp.concatenate(_split3(g_log), axis=0)
    gc = _dot(trib_ref[...], gstack)
    geb = _dot(gstack, e64_ref[...]).astype(BF16)
    bgc = jnp.where(is_beta_lane, beta, gc)
    hi = bgc.astype(BF16).astype(F32)
    rem = bgc - hi
    mid = rem.astype(BF16).astype(F32)
    piece = jnp.where(gate_lane < GATE_LANES, hi,
                      jnp.where(gate_lane < 2 * GATE_LANES, mid, rem - mid))
    x = _dot(piece.astype(BF16), eall3_ref[...])
    beta64 = x[:, :PACKED]
    gcol64 = x[:, PACKED:2 * PACKED]
    beta128 = x[:, 2 * PACKED:2 * PACKED + DN_WIDTH]
    gc128 = x[:, 2 * PACKED + DN_WIDTH:]
    ones_lhs = jnp.ones((2 * SUBLANES, 3 * CHUNK), BF16)

    st = {}

    def prepare(cs):
        for c in cs:
            rs = slice(c * CHUNK, (c + 1) * CHUNK)
            qn = _l2n_heads(conv_silu(c, 0), DN_HEAD_DIM ** -0.5)
            kn = _l2n_heads(conv_silu(c, 1), 1.0)
            pieces = [geb[p * rows + c * CHUNK:p * rows + (c + 1) * CHUNK] for p in range(3)]
            grow = _dot(ones_lhs, jnp.concatenate(pieces, axis=0) * u3_ref[...])[0:1]
            st[c] = dict(qb=qn.astype(BF16), kb=kn.astype(BF16), qn=qn, kn=kn,
                         vc=conv_silu(c, 2),
                         gdiff=gcol64[rs] - grow,
                         beta64=beta64[rs], beta128=beta128[rs], gc128=gc128[rs])
            yield
        sts = [st[c] for c in cs]

        for s in sts:
            qk, kk = [], []
            for g in range(N_GROUPS):
                gs = group_cols(g, DN_HEAD_DIM)
                bk = _block_diag(head_blocks(s["kb"], g))
                r = _dot_nt(jnp.concatenate([s["qb"][:, gs], s["kb"][:, gs]], axis=0), bk)
                qk.append(r[:CHUNK])
                kk.append(r[CHUNK:])
            s["qk"] = jnp.concatenate(qk, axis=1)
            s["kk"] = jnp.concatenate(kk, axis=1)
            s.pop("qb")
            s.pop("kb")
        yield

        for s in sts:
            decay = jnp.exp(jnp.where(causal_t, s.pop("gdiff"), -jnp.inf))
            a_mat = jnp.where(strict_t, s.pop("beta64") * s.pop("kk") * decay, 0.0)
            qkd = s.pop("qk") * decay
            s["qkd"] = [qkd[:, group_cols(g, CHUNK)].astype(BF16) for g in range(N_GROUPS)]
            s["w"] = [-a_mat[:, group_cols(g, CHUNK)] for g in range(N_GROUPS)]
            p0 = eye_t - a_mat
            s["p"] = [p0[:, group_cols(g, CHUNK)] for g in range(N_GROUPS)]
        yield

        for s in sts:
            for g in range(N_GROUPS):
                wb = s["w"][g].astype(BF16)
                s["w"][g] = _dot(wb, packed_block_diag(wb))
        yield
        n = 4
        while n < CHUNK:
            for s in sts:
                for g in range(N_GROUPS):
                    wb = s["w"][g].astype(BF16)
                    r = _dot(jnp.concatenate([wb, s["p"][g].astype(BF16)], axis=0),
                             packed_block_diag(wb))
                    s["w"][g] = r[:CHUNK]
                    s["p"][g] = s["p"][g] + r[CHUNK:]
            yield
            n *= 2
        for s in sts:
            for g in range(N_GROUPS):
                s["p"][g] = s["p"][g] + _dot(s["p"][g].astype(BF16),
                                             packed_block_diag(s["w"][g].astype(BF16)))
            s.pop("w")
        yield

        for s in sts:
            gc128_c = s.pop("gc128")
            beta128_c = s.pop("beta128")
            eg = jnp.exp(gc128_c)
            rv = (beta128_c * s.pop("vc")).astype(BF16)
            rk = ((beta128_c * eg) * s["kn"]).astype(BF16)
            us, ws = [], []
            for g in range(N_GROUPS):
                rhs = jnp.concatenate([_block_diag(head_blocks(rv, g)),
                                       _block_diag(head_blocks(rk, g))], axis=1)
                sol = _dot(s["p"][g].astype(BF16), rhs)
                us.append(sol[:, :GROUP_HEADS * DN_HEAD_DIM])
                ws.append(sol[:, GROUP_HEADS * DN_HEAD_DIM:])
            s.pop("p")
            s["u"] = jnp.concatenate(us, axis=1)
            w_all = jnp.concatenate(ws, axis=1)
            g_last = gc128_c[CHUNK - 1:CHUNK, :]
            q_dec = s.pop("qn") * eg
            s["wq"] = jnp.concatenate([w_all, q_dec], axis=0).astype(BF16)
            s["kdec"] = (s.pop("kn") * jnp.exp(g_last - gc128_c)).astype(BF16)
            s["sdecay"] = jnp.exp(g_last)
            yield

    state = [s_scr[h] for h in range(DN_HEADS)]
    pair = 2 * DN_HEAD_DIM

    def recurrence(cs):
        for c in cs:
            s = st.pop(c)
            r0 = c * CHUNK
            ws = []
            for p in range(DN_HEADS // 2):
                rhs = _block_diag([state[2 * p].astype(BF16), state[2 * p + 1].astype(BF16)])
                ws.append(_dot(s["wq"][:, p * pair:(p + 1) * pair], rhs))
            yield
            ws = jnp.concatenate(ws, axis=1)
            v_new = (s["u"] - ws[:CHUNK]).astype(BF16)
            o = []
            for g in range(N_GROUPS):
                o.append(ws[CHUNK:, group_cols(g, DN_HEAD_DIM)]
                         + _dot(s["qkd"][g], _block_diag(head_blocks(v_new, g))))
            upds = []
            for p in range(DN_HEADS // 2):
                ps = slice(p * pair, (p + 1) * pair)
                upds.append(_dot_tn(s["kdec"][:, ps], v_new[:, ps]))
            yield
            for p in range(DN_HEADS // 2):
                for i in range(2):
                    h = 2 * p + i
                    blk = slice(i * DN_HEAD_DIM, (i + 1) * DN_HEAD_DIM)
                    state[h] = state[h] * s["sdecay"][:, _head_cols(h)] + upds[p][blk, blk]
            o = jnp.concatenate(o, axis=1)
            for h in range(DN_HEADS):
                hs = _head_cols(h)
                zg = zg_ref[r0:r0 + CHUNK, hs].astype(F32)
                o_ref[r0:r0 + CHUNK, hs] = (_rms_norm(o[:, hs], norm_g) * zg).astype(BF16)

    def interleave(streams, shares):
        live = [True] * len(streams)
        while any(live):
            for i, stream in enumerate(streams):
                for _ in range(shares[i]):
                    if live[i]:
                        try:
                            next(stream)
                        except StopIteration:
                            live[i] = False

    groups = [list(range(i, i + DN_GROUP_CHUNKS))
              for i in range(0, DN_CHUNKS_PER_STEP, DN_GROUP_CHUNKS)]
    interleave([prepare(groups[0])], [1])
    for prev_group, group in zip(groups[:-1], groups[1:]):
        interleave([recurrence(prev_group), prepare(group)], [1, 2])
    interleave([recurrence(groups[-1])], [1])

    for h in range(DN_HEADS):
        s_scr[h] = state[h]
    for sec, ref in enumerate(raw_refs):
        xhalo[:, sec * DN_WIDTH:(sec + 1) * DN_WIDTH] = ref[rows - CONV_HALO:rows, :]


def _deltanet(big, ba, conv_w, alog_row, dtb_row, dn_norm_g, *, batch, seq):
    rows = DN_CHUNKS_PER_STEP * CHUNK
    steps = seq // rows
    n = batch * seq
    consts = _dn_constants()

    def tok_spec(col_block):
        return pl.BlockSpec((rows, DN_WIDTH), lambda b, t: (b * steps + t, col_block))

    def full_spec(a):
        return pl.BlockSpec(a.shape, lambda b, t: (0,) * a.ndim)

    return pl.pallas_call(
        _deltanet_kernel,
        grid=(batch, steps),
        in_specs=[
            tok_spec(0), tok_spec(1), tok_spec(2), tok_spec(3),
            pl.BlockSpec((rows, LANES), lambda b, t: (b * steps + t, 0)),
            full_spec(conv_w), full_spec(alog_row), full_spec(dtb_row), full_spec(dn_norm_g),
        ] + [full_spec(a) for a in consts],
        out_specs=pl.BlockSpec((rows, DN_WIDTH), lambda b, t: (b * steps + t, 0)),
        out_shape=jax.ShapeDtypeStruct((n, DN_WIDTH), BF16),
        scratch_shapes=[
            pltpu.VMEM((DN_HEADS, DN_HEAD_DIM, DN_HEAD_DIM), F32),
            pltpu.VMEM((CONV_HALO, 3 * DN_WIDTH), BF16),
        ],
        compiler_params=pltpu.CompilerParams(
            dimension_semantics=("arbitrary", "arbitrary"), vmem_limit_bytes=VMEM_LIMIT),
        name="deltanet",
    )(big, big, big, big, ba, conv_w, alog_row, dtb_row, dn_norm_g, *consts)


def _mix_out_kernel(x_ref, mod_ref, pooled_ref, gp_ref, gd_ref, og_ref, pw_ref, ps_ref,
                    pp_ref, dp_ref, wo_ref, o_ref, ya_scr):
    for gi in range(len(POOL_WINDOWS)):
        cols = slice(gi * POOL_GROUP_DIM, (gi + 1) * POOL_GROUP_DIM)
        ya_g = _dot(pooled_ref[:, cols], pw_ref[gi]) * ps_ref[:, cols]
        ya_scr[:, cols] = ya_g.astype(BF16)

    ya = _dot(ya_scr[...], pp_ref[...])
    yb = _dot(og_ref[...], dp_ref[...])
    merged = gp_ref[...].astype(F32) * ya + gd_ref[...].astype(F32) * yb
    out = _dot(merged.astype(BF16), wo_ref[...])
    res_gate = mod_ref[3 * 1 + 2:3 * 1 + 3, :]
    o_ref[...] = x_ref[...] + res_gate * out


def _mix_out(x2d, mods, big, og, pool_w, pool_scale, pool_proj, dn_proj, w_out, *, batch, seq):
    steps = seq // TM_OUT
    n = batch * seq

    def row_map(b, t):
        return b * steps + t

    return pl.pallas_call(
        _mix_out_kernel,
        grid=(batch, steps),
        in_specs=[
            pl.BlockSpec((TM_OUT, D_MODEL), lambda b, t: (row_map(b, t), 0)),
            pl.BlockSpec((None, MOD_ROWS, D_MODEL), lambda b, t: (b, 0, 0)),
            pl.BlockSpec((TM_OUT, POOL_WIDTH),
                         lambda b, t: (row_map(b, t), OUT_POOLED // POOL_WIDTH)),
            pl.BlockSpec((TM_OUT, D_MODEL), lambda b, t: (row_map(b, t), OUT_GATES // D_MODEL)),
            pl.BlockSpec((TM_OUT, D_MODEL),
                         lambda b, t: (row_map(b, t), OUT_GATES // D_MODEL + 1)),
            pl.BlockSpec((TM_OUT, DN_WIDTH), lambda b, t: (row_map(b, t), 0)),
            pl.BlockSpec((len(POOL_WINDOWS), POOL_GROUP_DIM, POOL_GROUP_DIM),
                         lambda b, t: (0, 0, 0)),
            pl.BlockSpec((1, POOL_WIDTH), lambda b, t: (0, 0)),
            pl.BlockSpec((POOL_WIDTH, D_MODEL), lambda b, t: (0, 0)),
            pl.BlockSpec((DN_WIDTH, D_MODEL), lambda b, t: (0, 0)),
            pl.BlockSpec((D_MODEL, D_MODEL), lambda b, t: (0, 0)),
        ],
        out_specs=pl.BlockSpec((TM_OUT, D_MODEL), lambda b, t: (row_map(b, t), 0)),
        out_shape=jax.ShapeDtypeStruct((n, D_MODEL), F32),
        scratch_shapes=[pltpu.VMEM((TM_OUT, POOL_WIDTH), BF16)],
        compiler_params=pltpu.CompilerParams(
            dimension_semantics=("arbitrary", "arbitrary"), vmem_limit_bytes=VMEM_LIMIT),
        name="mix_out",
    )(x2d, mods, big, big, big, og, pool_w, pool_scale, pool_proj, dn_proj, w_out)


def _layer(x2d, c_pad, ada_w, ada_b, norm_g, ffn1_w_in, ffn1_w_out, ffn2_w_in, ffn2_w_out,
           mix_w_in, conv_w, a_log, dt_bias, dn_norm_g, pool_w, pool_scale, pool_proj,
           dn_proj, mix_w_out, final_g, *, batch, seq, final):
    mod = _ada(c_pad, ada_w, ada_b[None, :])
    mods = mod[:batch].reshape(batch, 9, D_MODEL)
    mods = jnp.pad(mods, ((0, 0), (0, MOD_ROWS - 9), (0, 0)))

    fg = final_g[None, :]
    x2d, = _ffn(x2d, mods, norm_g[0][None, :], ffn1_w_in.astype(BF16), ffn1_w_out.astype(BF16),
                fg, sub=0, final=False, seq=seq)
    w_mix = mix_w_in.astype(BF16)

    gate_pad = LANES - GATE_COPIES * GATE_LANES
    w_tail = jnp.pad(w_mix[:, -GATE_LANES:], ((0, 0), (0, LANES - GATE_LANES)))
    n_pool = len(POOL_WINDOWS) * POOL_GROUP_DIM
    big, ba, w2_in, w2_out, w_dn, w_mo, w_pp, w_pw = _mix_in(
        x2d, mods, norm_g[1][None, :], w_mix, w_tail, batch=batch, seq=seq,
        cast=(ffn2_w_in, ffn2_w_out, dn_proj, mix_w_out, pool_proj,
              pool_w.reshape(n_pool, POOL_GROUP_DIM)))

    def alpha_row(v):
        return jnp.pad(jnp.tile(jnp.pad(v, (DN_HEADS, 0)), GATE_COPIES), (0, gate_pad))[None, :]

    alog_row = alpha_row(a_log)
    dtb_row = alpha_row(dt_bias)
    og = _deltanet(big, ba, conv_w, alog_row, dtb_row, dn_norm_g[None, :],
                   batch=batch, seq=seq)

    x2d = _mix_out(x2d, mods, big, og, w_pw.reshape(pool_w.shape), pool_scale[None, :],
                   w_pp, w_dn, w_mo, batch=batch, seq=seq)

    x2d, = _ffn(x2d, mods, norm_g[2][None, :], w2_in, w2_out, fg, sub=2, final=final, seq=seq)
    return x2d


def kernel(x, c, ada_w, ada_b, norm_g, ffn1_w_in, ffn1_w_out, ffn2_w_in, ffn2_w_out, mix_w_in, conv_w, a_log, dt_bias, dn_norm_g, pool_w, pool_scale, pool_proj, dn_proj, mix_w_out, final_g):
    batch, seq, d = x.shape
    depth = ada_w.shape[0]
    x2d = x.reshape(batch * seq, d)
    c_pad = jnp.pad(c, ((0, SUBLANES - batch), (0, 0)))
    for l in range(depth):
        x2d = _layer(x2d, c_pad, ada_w[l], ada_b[l], norm_g[l], ffn1_w_in[l], ffn1_w_out[l],
                     ffn2_w_in[l], ffn2_w_out[l], mix_w_in[l], conv_w[l], a_log[l],
                     dt_bias[l], dn_norm_g[l], pool_w[l], pool_scale[l], pool_proj[l],
                     dn_proj[l], mix_w_out[l], final_g,
                     batch=batch, seq=seq, final=(l == depth - 1))
    return x2d.reshape(batch, seq, d)
```

```python
import functools

import jax
import jax.numpy as jnp
import numpy as np
from jax import lax
from jax.experimental import pallas as pl
from jax.experimental.pallas import tpu as pltpu

F32 = jnp.float32
BF16 = jnp.bfloat16

D_MODEL = 1024
POOL_WINDOWS = (2, 4, 8, 16)
POOL_GROUP_DIM = 128
POOL_WIDTH = 512
DN_HEAD_DIM = 128
DN_HEADS = 8
DN_WIDTH = 1024
CONV_WIDTH = 4
CHUNK = 64
FFN_HIDDEN = 2816
RMS_EPS = 1e-6
L2_EPS = 1e-6

LANES = 128
SUBLANES = 8
BF16_ROWS = 16
VMEM_LIMIT = 56 * 1024 * 1024

TM_FFN = 1024
TH_FFN = 256
TM_MIX = 512
TN_MIX = 512
EPI_ROWS = 64
DN_CHUNKS_PER_STEP = 8
DN_GROUP_CHUNKS = 4
GROUP_HEADS = 4
N_GROUPS = DN_HEADS // GROUP_HEADS
PACK = GROUP_HEADS * CHUNK
PACKED = DN_HEADS * CHUNK
TM_OUT = 1024
POOL_HALO = 16
CONV_HALO = 16
MOD_ROWS = 16
GATE_LANES = 2 * DN_HEADS
GATE_COPIES = 3


def _dot(a, b):
    return jnp.dot(a, b, preferred_element_type=F32)


def _dot_nt(a, b):
    return lax.dot_general(a, b, (((1,), (1,)), ((), ())), preferred_element_type=F32)


def _dot_tn(a, b):
    return lax.dot_general(a, b, (((0,), (0,)), ((), ())), preferred_element_type=F32)


def _sigmoid(x):
    return jax.nn.sigmoid(x)


def _silu(x):
    return x * _sigmoid(x)


def _rms_norm(x, g):
    ms = jnp.mean(x * x, axis=-1, keepdims=True)
    return (x * lax.rsqrt(ms + RMS_EPS)) * g


def _modulated_norm(x, g, mod_ref, sub):
    shift = mod_ref[3 * sub + 0:3 * sub + 1, :]
    scale = mod_ref[3 * sub + 1:3 * sub + 2, :]
    return _rms_norm(x, g) * (1.0 + scale) + shift


def _const_spec(shape):
    nd = len(shape)
    return pl.BlockSpec(shape, lambda *_: (0,) * nd, pipeline_mode=pl.Buffered(1))


def _side_cast_specs(arrays, n_steps, step_of):
    in_specs, out_specs, out_shapes = [], [], []
    for a in arrays:
        rows, cols = a.shape
        n_blocks = n_steps
        while rows % n_blocks or (rows // n_blocks) % BF16_ROWS:
            n_blocks //= 2
        repeat = n_steps // n_blocks
        spec = pl.BlockSpec((rows // n_blocks, cols),
                            lambda *idx, repeat=repeat: (step_of(*idx) // repeat, 0))
        in_specs.append(spec)
        out_specs.append(spec)
        out_shapes.append(jax.ShapeDtypeStruct(a.shape, BF16))
    return in_specs, out_specs, out_shapes


def _side_cast(in_refs, out_refs):
    for src_ref, dst_ref in zip(in_refs, out_refs):
        dst_ref[...] = src_ref[...].astype(BF16)


def _ada_kernel(c_ref, w_ref, b_ref, o_ref):
    s = _silu(c_ref[...]).astype(BF16)
    o_ref[...] = _dot(s, w_ref[...].astype(BF16)) + b_ref[...]


def _ada(c_pad, ada_w, ada_b):
    n = ada_w.shape[1]
    tn = n // 4
    return pl.pallas_call(
        _ada_kernel,
        grid=(n // tn,),
        in_specs=[
            pl.BlockSpec((SUBLANES, D_MODEL), lambda j: (0, 0)),
            pl.BlockSpec((D_MODEL, tn), lambda j: (0, j)),
            pl.BlockSpec((1, tn), lambda j: (0, j)),
        ],
        out_specs=pl.BlockSpec((SUBLANES, tn), lambda j: (0, j)),
        out_shape=jax.ShapeDtypeStruct((SUBLANES, n), F32),
        compiler_params=pltpu.CompilerParams(
            dimension_semantics=("arbitrary",), vmem_limit_bytes=VMEM_LIMIT),
        name="ada",
    )(c_pad, ada_w, ada_b)


def _ffn_kernel(*refs, sub, final, n_cast):
    x_ref, mod_ref, g_ref, wi_ref, wo_ref, fg_ref = refs[:6]
    cast_in = refs[6:6 + n_cast]
    o_ref = refs[6 + n_cast]
    cast_out = refs[7 + n_cast:7 + 2 * n_cast]
    h_scr, acc_scr = refs[7 + 2 * n_cast:]
    _side_cast(cast_in, cast_out)
    x = x_ref[...]
    h_scr[...] = _modulated_norm(x, g_ref[...], mod_ref, sub).astype(BF16)
    n_chunks = FFN_HIDDEN // TH_FFN
    for j in range(n_chunks):
        cols = slice(j * TH_FFN, (j + 1) * TH_FFN)
        up_cols = slice(FFN_HIDDEN + j * TH_FFN, FFN_HIDDEN + (j + 1) * TH_FFN)
        h = h_scr[...]
        gate = _dot(h, wi_ref[:, cols])
        up = _dot(h, wi_ref[:, up_cols])
        act = (_silu(gate) * up).astype(BF16)
        part = _dot(act, wo_ref[cols, :])
        if j == 0:
            acc_scr[...] = part
        else:
            acc_scr[...] += part
    res_gate = mod_ref[3 * sub + 2:3 * sub + 3, :]
    y = x_ref[...] + (0.5 * res_gate) * acc_scr[...]
    if final:
        y = _rms_norm(y, fg_ref[...])
    o_ref[...] = y


def _ffn(x2d, mods, norm_g, w_in, w_out, final_g, *, sub, final, seq, cast=()):
    n = x2d.shape[0]
    tiles_per_seq = seq // TM_FFN
    cast_in, cast_out, cast_shapes = _side_cast_specs(cast, n // TM_FFN, lambda i: i)
    return pl.pallas_call(
        functools.partial(_ffn_kernel, sub=sub, final=final, n_cast=len(cast)),
        grid=(n // TM_FFN,),
        in_specs=[
            pl.BlockSpec((TM_FFN, D_MODEL), lambda i: (i, 0)),
            pl.BlockSpec((None, MOD_ROWS, D_MODEL), lambda i: (i // tiles_per_seq, 0, 0)),
            _const_spec((1, D_MODEL)),
            _const_spec((D_MODEL, 2 * FFN_HIDDEN)),
            _const_spec((FFN_HIDDEN, D_MODEL)),
            _const_spec((1, D_MODEL)),
        ] + cast_in,
        out_specs=[pl.BlockSpec((TM_FFN, D_MODEL), lambda i: (i, 0))] + cast_out,
        out_shape=[jax.ShapeDtypeStruct((n, D_MODEL), F32)] + cast_shapes,
        scratch_shapes=[
            pltpu.VMEM((TM_FFN, D_MODEL), BF16),
            pltpu.VMEM((TM_FFN, D_MODEL), F32),
        ],
        compiler_params=pltpu.CompilerParams(
            dimension_semantics=("arbitrary",), vmem_limit_bytes=VMEM_LIMIT),
        name="ffn%d" % sub,
    )(x2d, mods, norm_g, w_in, w_out, final_g, *cast)


def _mix_in_kernel(*refs, n_cast):
    x_ref, mod_ref, g_ref, w_ref, wtail_ref = refs[:5]
    cast_in = refs[5:5 + n_cast]
    o_ref, ba_ref = refs[5 + n_cast:7 + n_cast]
    cast_out = refs[7 + n_cast:7 + 2 * n_cast]
    h_scr, phalo = refs[7 + 2 * n_cast:]
    _mix_in_body(x_ref, mod_ref, g_ref, w_ref, wtail_ref, o_ref, ba_ref, h_scr, phalo)
    _side_cast(cast_in, cast_out)


def _mix_in_body(x_ref, mod_ref, g_ref, w_ref, wtail_ref, o_ref, ba_ref, h_scr, phalo):
    t = pl.program_id(1)

    @pl.when(t == 0)
    def _():
        phalo[...] = jnp.zeros(phalo.shape, F32)

    h_scr[...] = _modulated_norm(x_ref[...], g_ref[...], mod_ref, 1).astype(BF16)

    sub_pool = lax.broadcasted_iota(jnp.int32, (SUBLANES, POOL_GROUP_DIM), 0)

    def shift_rows(xb, s, sub):
        n = xb.shape[0] // SUBLANES
        rots = [pltpu.roll(xb[SUBLANES * k:SUBLANES * (k + 1)], s, axis=0) for k in range(n)]
        return jnp.concatenate(
            [jnp.where(sub < s, rots[k - 1], rots[k]) for k in range(1, n)], axis=0)

    def pool_block(acc, out0, gi, r0):
        win = POOL_WINDOWS[gi]
        lanes = slice(gi * POOL_GROUP_DIM, (gi + 1) * POOL_GROUP_DIM)
        gl = slice(out0 + gi * POOL_GROUP_DIM, out0 + (gi + 1) * POOL_GROUP_DIM)
        if r0 == 0:
            xb = jnp.concatenate([phalo[:, lanes], acc[0:EPI_ROWS, lanes]], axis=0)
            phalo[:, lanes] = acc[TM_MIX - POOL_HALO:, lanes]
        else:
            xb = acc[r0 - POOL_HALO:r0 + EPI_ROWS, lanes]
        x0 = xb[POOL_HALO:]
        wsum = xb[SUBLANES:]
        if win > SUBLANES:
            wsum = wsum + xb[:-SUBLANES]
        lag = 1
        while lag < min(win, SUBLANES):
            prev = jnp.concatenate([xb[:SUBLANES], wsum], axis=0)
            wsum = wsum + shift_rows(prev, lag, sub_pool)
            lag *= 2
        wsum = wsum[SUBLANES:]
        pos = (t * TM_MIX + r0 + 1
               + lax.broadcasted_iota(jnp.int32, (EPI_ROWS, 1), 0)).astype(F32)
        pooled = wsum / jnp.minimum(pos, float(win)) - x0
        o_ref[r0:r0 + EPI_ROWS, gl] = pooled.astype(BF16)

    def project(col0):
        return _dot(h_scr[...], w_ref[:, col0:col0 + TN_MIX])

    acc = project(0)
    for gi in range(len(POOL_WINDOWS)):
        for r0 in range(0, TM_MIX, EPI_ROWS):
            pool_block(acc, 6 * D_MODEL, gi, r0)

    for j in range(4 * DN_WIDTH // TN_MIX):
        cols = slice(j * TN_MIX, (j + 1) * TN_MIX)
        acc = project(POOL_WIDTH + j * TN_MIX)
        if (j * TN_MIX) // DN_WIDTH < 3:
            o_ref[:, cols] = acc.astype(BF16)
        else:
            o_ref[:, cols] = _silu(acc).astype(BF16)

    g0 = POOL_WIDTH + 4 * DN_WIDTH
    prev = project(g0)
    slab = prev[:, :LANES]
    lane = lax.broadcasted_iota(jnp.int32, slab.shape, 1)
    copies = jnp.zeros_like(slab)
    for p in reversed(range(GATE_COPIES)):
        shifted = slab if p == 0 else pltpu.roll(slab, p * GATE_LANES, axis=1)
        copies = jnp.where(lane < (p + 1) * GATE_LANES, shifted, copies)
    ba_ref[...] = copies
    for j in range(2 * D_MODEL // TN_MIX):
        if j == 2 * D_MODEL // TN_MIX - 1:
            nxt = _dot(h_scr[...], wtail_ref[...])
        else:
            nxt = project(g0 + (j + 1) * TN_MIX)
        gates = jnp.concatenate([prev[:, GATE_LANES:], nxt[:, :GATE_LANES]], axis=1)
        o_ref[:, 4 * DN_WIDTH + j * TN_MIX:4 * DN_WIDTH + (j + 1) * TN_MIX] = (
            _sigmoid(gates).astype(BF16))
        prev = nxt


def _mix_in(x2d, mods, norm_g, w_mix, w_tail, *, batch, seq, cast=()):
    n = x2d.shape[0]
    n_out = 6 * D_MODEL + POOL_WIDTH
    steps = seq // TM_MIX
    assert TN_MIX == POOL_WIDTH and w_mix.shape[1] == n_out + GATE_LANES
    cast_in, cast_out, cast_shapes = _side_cast_specs(
        cast, batch * steps, lambda b, t: b * steps + t)
    return pl.pallas_call(
        functools.partial(_mix_in_kernel, n_cast=len(cast)),
        grid=(batch, steps),
        in_specs=[
            pl.BlockSpec((TM_MIX, D_MODEL), lambda b, t: (b * steps + t, 0)),
            pl.BlockSpec((None, MOD_ROWS, D_MODEL), lambda b, t: (b, 0, 0)),
            _const_spec((1, D_MODEL)),
            _const_spec(w_mix.shape),
            _const_spec((D_MODEL, LANES)),
        ] + cast_in,
        out_specs=[
            pl.BlockSpec((TM_MIX, n_out), lambda b, t: (b * steps + t, 0)),
            pl.BlockSpec((TM_MIX, LANES), lambda b, t: (b * steps + t, 0)),
        ] + cast_out,
        out_shape=[
            jax.ShapeDtypeStruct((n, n_out), BF16),
            jax.ShapeDtypeStruct((n, LANES), F32),
        ] + cast_shapes,
        scratch_shapes=[
            pltpu.VMEM((TM_MIX, D_MODEL), BF16),
            pltpu.VMEM((POOL_HALO, POOL_WIDTH), F32),
        ],
        compiler_params=pltpu.CompilerParams(
            dimension_semantics=("arbitrary", "arbitrary"), vmem_limit_bytes=VMEM_LIMIT),
        name="mix_in",
    )(x2d, mods, norm_g, w_mix, w_tail, *cast)


def _split3(x):
    hi = x.astype(BF16)
    r = x - hi.astype(F32)
    mid = r.astype(BF16)
    lo = (r - mid.astype(F32)).astype(BF16)
    return hi, mid, lo


def _block_diag(blocks):
    n = len(blocks)
    zero = jnp.zeros_like(blocks[0])
    rows = [jnp.concatenate([blocks[i] if j == i else zero for j in range(n)], axis=1)
            for i in range(n)]
    return jnp.concatenate(rows, axis=0)


def _head_cols(h, width=DN_HEAD_DIM):
    return slice(h * width, (h + 1) * width)


def _l2n_heads(x, scale):
    outs = []
    for h in range(x.shape[1] // DN_HEAD_DIM):
        xh = x[:, _head_cols(h)]
        inv = lax.rsqrt(jnp.sum(xh * xh, axis=-1, keepdims=True) + L2_EPS)
        outs.append(xh * (inv * scale))
    return jnp.concatenate(outs, axis=1)


def _dn_constants():
    tri = np.tril(np.ones((CHUNK, CHUNK), np.float32))
    tri_blk = np.tile(np.kron(np.eye(DN_CHUNKS_PER_STEP, dtype=np.float32), tri), (1, 3))
    e64 = np.zeros((LANES, PACKED), np.float32)
    eb64 = np.zeros((LANES, PACKED), np.float32)
    e128 = np.zeros((LANES, DN_WIDTH), np.float32)
    eb128 = np.zeros((LANES, DN_WIDTH), np.float32)
    for h in range(DN_HEADS):
        e64[DN_HEADS + h, _head_cols(h, CHUNK)] = 1.0
        eb64[h, _head_cols(h, CHUNK)] = 1.0
        e128[DN_HEADS + h, _head_cols(h)] = 1.0
        eb128[h, _head_cols(h)] = 1.0
    eall = np.concatenate([eb64, e64, eb128, e128], axis=1)
    eall3 = np.zeros_like(eall)
    for p in range(GATE_COPIES):
        eall3[p * GATE_LANES:(p + 1) * GATE_LANES] = eall[:GATE_LANES]
    r = np.arange(CHUNK)[:, None]
    c = np.arange(PACKED)[None, :] % CHUNK
    u3 = np.tile((r <= c).astype(np.float32), (3, 1))
    rr = np.arange(PACK)[:, None] // CHUNK
    cc = np.arange(PACK)[None, :] // CHUNK
    bdm = (rr == cc).astype(np.float32)
    dshift = np.zeros(((CONV_WIDTH - 1) * CHUNK, CONV_HALO + CHUNK), np.float32)
    for s in range(1, CONV_WIDTH):
        dshift[(s - 1) * CHUNK + np.arange(CHUNK), CONV_HALO + np.arange(CHUNK) - s] = 1.0
    return tuple(jnp.asarray(a, BF16) for a in (tri_blk, e64, eall3, u3, bdm, dshift))


def _deltanet_kernel(q_ref, k_ref, v_ref, zg_ref, ba_ref, cw_ref, alog_ref, dtb_ref, ng_ref,
                     trib_ref, e64_ref, eall3_ref, u3_ref, bdm_ref, dshift_ref,
                     o_ref, s_scr, xhalo):
    t = pl.program_id(1)
    rows = DN_CHUNKS_PER_STEP * CHUNK

    @pl.when(t == 0)
    def _():
        s_scr[...] = jnp.zeros(s_scr.shape, F32)
        xhalo[...] = jnp.zeros(xhalo.shape, BF16)

    raw_refs = (q_ref, k_ref, v_ref)

    def conv_silu(c, sec):
        ref = raw_refs[sec]
        cols = slice(sec * DN_WIDTH, (sec + 1) * DN_WIDTH)
        if c == 0:
            xe = jnp.concatenate([xhalo[:, cols], ref[0:CHUNK, :]], axis=0)
        else:
            xe = ref[c * CHUNK - CONV_HALO:(c + 1) * CHUNK, :]
        shifted = _dot(dshift_ref[...], xe)
        y = cw_ref[CONV_WIDTH - 1:CONV_WIDTH, cols] * xe[CONV_HALO:].astype(F32)
        for s in range(1, CONV_WIDTH):
            tap = CONV_WIDTH - 1 - s
            y = y + cw_ref[tap:tap + 1, cols] * shifted[(s - 1) * CHUNK:s * CHUNK]
        return _silu(y)

    prow = lax.broadcasted_iota(jnp.int32, (CHUNK, PACKED), 0)
    pcol = lax.broadcasted_iota(jnp.int32, (CHUNK, PACKED), 1) & (CHUNK - 1)
    causal_t = prow >= pcol
    strict_t = prow > pcol
    eye_t = jnp.where(prow == pcol, 1.0, 0.0).astype(F32)
    gate_lane = lax.broadcasted_iota(jnp.int32, (rows, LANES), 1)
    is_beta_lane = (gate_lane & (GATE_LANES - 1)) < DN_HEADS

    neg_decay_rate = -jnp.exp(alog_ref[...])
    dt_bias = dtb_ref[...]
    norm_g = ng_ref[...]
    bdm = bdm_ref[...]

    def group_cols(g, width):
        return slice(g * GROUP_HEADS * width, (g + 1) * GROUP_HEADS * width)

    def head_blocks(x, g):
        return [x[:, _head_cols(g * GROUP_HEADS + i)] for i in range(GROUP_HEADS)]

    def packed_block_diag(wb):
        return jnp.concatenate([wb] * GROUP_HEADS, axis=0) * bdm

    ba = ba_ref[...]
    beta = _sigmoid(ba)
    xa = ba + dt_bias
    softplus = jnp.maximum(xa, 0.0) + jnp.log1p(jnp.exp(-jnp.abs(xa)))
    g_log = neg_decay_rate * softplus

    gstack = jnp.concatenate(_split3(g_log), axis=0)
    gc = _dot(trib_ref[...], gstack)
    geb = _dot(gstack, e64_ref[...]).astype(BF16)
    bgc = jnp.where(is_beta_lane, beta, gc)
    hi = bgc.astype(BF16).astype(F32)
    rem = bgc - hi
    mid = rem.astype(BF16).astype(F32)
    piece = jnp.where(gate_lane < GATE_LANES, hi,
                      jnp.where(gate_lane < 2 * GATE_LANES, mid, rem - mid))
    x = _dot(piece.astype(BF16), eall3_ref[...])
    beta64 = x[:, :PACKED]
    gcol64 = x[:, PACKED:2 * PACKED]
    beta128 = x[:, 2 * PACKED:2 * PACKED + DN_WIDTH]
    gc128 = x[:, 2 * PACKED + DN_WIDTH:]
    ones_lhs = jnp.ones((2 * SUBLANES, 3 * CHUNK), BF16)

    st = {}

    def prepare(cs):
        for c in cs:
            rs = slice(c * CHUNK, (c + 1) * CHUNK)
            qn = _l2n_heads(conv_silu(c, 0), DN_HEAD_DIM ** -0.5)
            kn = _l2n_heads(conv_silu(c, 1), 1.0)
            pieces = [geb[p * rows + c * CHUNK:p * rows + (c + 1) * CHUNK] for p in range(3)]
            grow = _dot(ones_lhs, jnp.concatenate(pieces, axis=0) * u3_ref[...])[0:1]
            st[c] = dict(qb=qn.astype(BF16), kb=kn.astype(BF16), qn=qn, kn=kn,
                         vc=conv_silu(c, 2),
                         gdiff=gcol64[rs] - grow,
                         beta64=beta64[rs], beta128=beta128[rs], gc128=gc128[rs])
            yield
        sts = [st[c] for c in cs]

        for s in sts:
            qk, kk = [], []
            for g in range(N_GROUPS):
                gs = group_cols(g, DN_HEAD_DIM)
                bk = _block_diag(head_blocks(s["kb"], g))
                r = _dot_nt(jnp.concatenate([s["qb"][:, gs], s["kb"][:, gs]], axis=0), bk)
                qk.append(r[:CHUNK])
                kk.append(r[CHUNK:])
            s["qk"] = jnp.concatenate(qk, axis=1)
            s["kk"] = jnp.concatenate(kk, axis=1)
            s.pop("qb")
            s.pop("kb")
        yield

        for s in sts:
            decay = jnp.exp(jnp.where(causal_t, s.pop("gdiff"), -jnp.inf))
            a_mat = jnp.where(strict_t, s.pop("beta64") * s.pop("kk") * decay, 0.0)
            qkd = s.pop("qk") * decay
            s["qkd"] = [qkd[:, group_cols(g, CHUNK)].astype(BF16) for g in range(N_GROUPS)]
            s["w"] = [-a_mat[:, group_cols(g, CHUNK)] for g in range(N_GROUPS)]
            p0 = eye_t - a_mat
            s["p"] = [p0[:, group_cols(g, CHUNK)] for g in range(N_GROUPS)]
        yield

        for s in sts:
            for g in range(N_GROUPS):
                wb = s["w"][g].astype(BF16)
                s["w"][g] = _dot(wb, packed_block_diag(wb))
        yield
        n = 4
        while n < CHUNK:
            for s in sts:
                for g in range(N_GROUPS):
                    wb = s["w"][g].astype(BF16)
                    r = _dot(jnp.concatenate([wb, s["p"][g].astype(BF16)], axis=0),
                             packed_block_diag(wb))
                    s["w"][g] = r[:CHUNK]
                    s["p"][g] = s["p"][g] + r[CHUNK:]
            yield
            n *= 2
        for s in sts:
            for g in range(N_GROUPS):
                s["p"][g] = s["p"][g] + _dot(s["p"][g].astype(BF16),
                                             packed_block_diag(s["w"][g].astype(BF16)))
            s.pop("w")
        yield

        for s in sts:
            gc128_c = s.pop("gc128")
            beta128_c = s.pop("beta128")
            eg = jnp.exp(gc128_c)
            rv = (beta128_c * s.pop("vc")).astype(BF16)
            rk = ((beta128_c * eg) * s["kn"]).astype(BF16)
            us, ws = [], []
            for g in range(N_GROUPS):
                rhs = jnp.concatenate([_block_diag(head_blocks(rv, g)),
                                       _block_diag(head_blocks(rk, g))], axis=1)
                sol = _dot(s["p"][g].astype(BF16), rhs)
                us.append(sol[:, :GROUP_HEADS * DN_HEAD_DIM])
                ws.append(sol[:, GROUP_HEADS * DN_HEAD_DIM:])
            s.pop("p")
            s["u"] = jnp.concatenate(us, axis=1)
            w_all = jnp.concatenate(ws, axis=1)
            g_last = gc128_c[CHUNK - 1:CHUNK, :]
            q_dec = s.pop("qn") * eg
            s["wq"] = jnp.concatenate([w_all, q_dec], axis=0).astype(BF16)
            s["kdec"] = (s.pop("kn") * jnp.exp(g_last - gc128_c)).astype(BF16)
            s["sdecay"] = jnp.exp(g_last)
            yield

    state = [s_scr[h] for h in range(DN_HEADS)]
    pair = 2 * DN_HEAD_DIM

    def recurrence(cs):
        for c in cs:
            s = st.pop(c)
            r0 = c * CHUNK
            ws = []
            for p in range(DN_HEADS // 2):
                rhs = _block_diag([state[2 * p].astype(BF16), state[2 * p + 1].astype(BF16)])
                ws.append(_dot(s["wq"][:, p * pair:(p + 1) * pair], rhs))
            yield
            ws = jnp.concatenate(ws, axis=1)
            v_new = (s["u"] - ws[:CHUNK]).astype(BF16)
            o = []
            for g in range(N_GROUPS):
                o.append(ws[CHUNK:, group_cols(g, DN_HEAD_DIM)]
                         + _dot(s["qkd"][g], _block_diag(head_blocks(v_new, g))))
            upds = []
            for p in range(DN_HEADS // 2):
                ps = slice(p * pair, (p + 1) * pair)
                upds.append(_dot_tn(s["kdec"][:, ps], v_new[:, ps]))
            yield
            for p in range(DN_HEADS // 2):
                for i in range(2):
                    h = 2 * p + i
                    blk = slice(i * DN_HEAD_DIM, (i + 1) * DN_HEAD_DIM)
                    state[h] = state[h] * s["sdecay"][:, _head_cols(h)] + upds[p][blk, blk]
            o = jnp.concatenate(o, axis=1)
            for h in range(DN_HEADS):
                hs = _head_cols(h)
                zg = zg_ref[r0:r0 + CHUNK, hs].astype(F32)
                o_ref[r0:r0 + CHUNK, hs] = (_rms_norm(o[:, hs], norm_g) * zg).astype(BF16)

    def interleave(streams, shares):
        live = [True] * len(streams)
        while any(live):
            for i, stream in enumerate(streams):
                for _ in range(shares[i]):
                    if live[i]:
                        try:
                            next(stream)
                        except StopIteration:
                            live[i] = False

    groups = [list(range(i, i + DN_GROUP_CHUNKS))
              for i in range(0, DN_CHUNKS_PER_STEP, DN_GROUP_CHUNKS)]
    interleave([prepare(groups[0])], [1])
    for prev_group, group in zip(groups[:-1], groups[1:]):
        interleave([recurrence(prev_group), prepare(group)], [1, 3])
    interleave([recurrence(groups[-1])], [1])

    for h in range(DN_HEADS):
        s_scr[h] = state[h]
    for sec, ref in enumerate(raw_refs):
        xhalo[:, sec * DN_WIDTH:(sec + 1) * DN_WIDTH] = ref[rows - CONV_HALO:rows, :]


def _deltanet(big, ba, conv_w, alog_row, dtb_row, dn_norm_g, *, batch, seq):
    rows = DN_CHUNKS_PER_STEP * CHUNK
    steps = seq // rows
    n = batch * seq
    consts = _dn_constants()

    def tok_spec(col_block):
        return pl.BlockSpec((rows, DN_WIDTH), lambda b, t: (b * steps + t, col_block))

    def full_spec(a):
        return pl.BlockSpec(a.shape, lambda b, t: (0,) * a.ndim)

    return pl.pallas_call(
        _deltanet_kernel,
        grid=(batch, steps),
        in_specs=[
            tok_spec(0), tok_spec(1), tok_spec(2), tok_spec(3),
            pl.BlockSpec((rows, LANES), lambda b, t: (b * steps + t, 0)),
            full_spec(conv_w), full_spec(alog_row), full_spec(dtb_row), full_spec(dn_norm_g),
        ] + [full_spec(a) for a in consts],
        out_specs=pl.BlockSpec((rows, DN_WIDTH), lambda b, t: (b * steps + t, 0)),
        out_shape=jax.ShapeDtypeStruct((n, DN_WIDTH), BF16),
        scratch_shapes=[
            pltpu.VMEM((DN_HEADS, DN_HEAD_DIM, DN_HEAD_DIM), F32),
            pltpu.VMEM((CONV_HALO, 3 * DN_WIDTH), BF16),
        ],
        compiler_params=pltpu.CompilerParams(
            dimension_semantics=("arbitrary", "arbitrary"), vmem_limit_bytes=VMEM_LIMIT),
        name="deltanet",
    )(big, big, big, big, ba, conv_w, alog_row, dtb_row, dn_norm_g, *consts)


def _pool_fold_kernel(pw_ref, ps_ref, pp_ref, o_ref):
    def pieces(a):
        hi = a.astype(BF16)
        return hi, (a - hi.astype(F32)).astype(BF16)

    for gi in range(len(POOL_WINDOWS)):
        rows = slice(gi * POOL_GROUP_DIM, (gi + 1) * POOL_GROUP_DIM)
        a_hi, a_lo = pieces(pw_ref[gi] * ps_ref[:, rows])
        b_hi, b_lo = pieces(pp_ref[rows, :])
        o_ref[rows, :] = (_dot(a_hi, b_hi) + _dot(a_hi, b_lo) + _dot(a_lo, b_hi)).astype(BF16)


def _pool_fold(pool_w, pool_scale, pool_proj):
    return pl.pallas_call(
        _pool_fold_kernel,
        out_shape=jax.ShapeDtypeStruct((POOL_WIDTH, D_MODEL), BF16),
        compiler_params=pltpu.CompilerParams(vmem_limit_bytes=VMEM_LIMIT),
        name="pool_fold",
    )(pool_w, pool_scale, pool_proj)


def _mix_out_kernel(x_ref, mod_ref, pooled_ref, gp_ref, gd_ref, og_ref, pf_ref, dp_ref, wo_ref,
                    o_ref):
    ya = _dot(pooled_ref[...], pf_ref[...])
    yb = _dot(og_ref[...], dp_ref[...])
    merged = gp_ref[...].astype(F32) * ya + gd_ref[...].astype(F32) * yb
    out = _dot(merged.astype(BF16), wo_ref[...])
    res_gate = mod_ref[5:6, :]
    o_ref[...] = x_ref[...] + res_gate * out


def _mix_out(x2d, mods, big, og, pool_fold, dn_proj, w_out, *, batch, seq):
    steps = seq // TM_OUT
    n = batch * seq

    def row_map(b, t):
        return b * steps + t

    return pl.pallas_call(
        _mix_out_kernel,
        grid=(batch, steps),
        in_specs=[
            pl.BlockSpec((TM_OUT, D_MODEL), lambda b, t: (row_map(b, t), 0)),
            pl.BlockSpec((None, MOD_ROWS, D_MODEL), lambda b, t: (b, 0, 0)),
            pl.BlockSpec((TM_OUT, POOL_WIDTH), lambda b, t: (row_map(b, t), 12)),
            pl.BlockSpec((TM_OUT, D_MODEL), lambda b, t: (row_map(b, t), 4)),
            pl.BlockSpec((TM_OUT, D_MODEL), lambda b, t: (row_map(b, t), 5)),
            pl.BlockSpec((TM_OUT, DN_WIDTH), lambda b, t: (row_map(b, t), 0)),
            pl.BlockSpec((POOL_WIDTH, D_MODEL), lambda b, t: (0, 0)),
            pl.BlockSpec((DN_WIDTH, D_MODEL), lambda b, t: (0, 0)),
            pl.BlockSpec((D_MODEL, D_MODEL), lambda b, t: (0, 0)),
        ],
        out_specs=pl.BlockSpec((TM_OUT, D_MODEL), lambda b, t: (row_map(b, t), 0)),
        out_shape=jax.ShapeDtypeStruct((n, D_MODEL), F32),
        compiler_params=pltpu.CompilerParams(
            dimension_semantics=("arbitrary", "arbitrary"), vmem_limit_bytes=VMEM_LIMIT),
        name="mix_out",
    )(x2d, mods, big, big, big, og, pool_fold, dn_proj, w_out)


def _layer(x2d, c_pad, ada_w, ada_b, norm_g, ffn1_w_in, ffn1_w_out, ffn2_w_in, ffn2_w_out,
           mix_w_in, conv_w, a_log, dt_bias, dn_norm_g, pool_w, pool_scale, pool_proj,
           dn_proj, mix_w_out, final_g, *, batch, seq, final):
    mod = _ada(c_pad, ada_w, ada_b[None, :])
    mods = mod[:batch].reshape(batch, 9, D_MODEL)
    mods = jnp.pad(mods, ((0, 0), (0, MOD_ROWS - 9), (0, 0)))

    fg = final_g[None, :]
    x2d, = _ffn(x2d, mods, norm_g[0][None, :], ffn1_w_in.astype(BF16), ffn1_w_out.astype(BF16),
                fg, sub=0, final=False, seq=seq)
    w_mix = mix_w_in.astype(BF16)

    gate_pad = LANES - GATE_COPIES * GATE_LANES
    w_tail = jnp.pad(w_mix[:, -GATE_LANES:], ((0, 0), (0, LANES - GATE_LANES)))
    big, ba, w2_in, w2_out, w_dn, w_mo = _mix_in(
        x2d, mods, norm_g[1][None, :], w_mix, w_tail, batch=batch, seq=seq,
        cast=(ffn2_w_in, ffn2_w_out, dn_proj, mix_w_out))

    def alpha_row(v):
        return jnp.pad(jnp.tile(jnp.pad(v, (DN_HEADS, 0)), GATE_COPIES), (0, gate_pad))[None, :]

    alog_row = alpha_row(a_log)
    dtb_row = alpha_row(dt_bias)
    og = _deltanet(big, ba, conv_w, alog_row, dtb_row, dn_norm_g[None, :],
                   batch=batch, seq=seq)

    x2d = _mix_out(x2d, mods, big, og, _pool_fold(pool_w, pool_scale[None, :], pool_proj),
                   w_dn, w_mo, batch=batch, seq=seq)

    x2d, = _ffn(x2d, mods, norm_g[2][None, :], w2_in, w2_out, fg, sub=2, final=final, seq=seq)
    return x2d


def kernel(x, c, ada_w, ada_b, norm_g, ffn1_w_in, ffn1_w_out, ffn2_w_in, ffn2_w_out, mix_w_in, conv_w, a_log, dt_bias, dn_norm_g, pool_w, pool_scale, pool_proj, dn_proj, mix_w_out, final_g):
    batch, seq, d = x.shape
    depth = ada_w.shape[0]
    x2d = x.reshape(batch * seq, d)
    c_pad = jnp.pad(c, ((0, SUBLANES - batch), (0, 0)))
    for l in range(depth):
        x2d = _layer(x2d, c_pad, ada_w[l], ada_b[l], norm_g[l], ffn1_w_in[l], ffn1_w_out[l],
                     ffn2_w_in[l], ffn2_w_out[l], mix_w_in[l], conv_w[l], a_log[l],
                     dt_bias[l], dn_norm_g[l], pool_w[l], pool_scale[l], pool_proj[l],
                     dn_proj[l], mix_w_out[l], final_g,
                     batch=batch, seq=seq, final=(l == depth - 1))
    return x2d.reshape(batch, seq, d)
```

```python
import functools

import jax
import jax.numpy as jnp
import numpy as np
from jax import lax
from jax.experimental import pallas as pl
from jax.experimental.pallas import tpu as pltpu

F32 = jnp.float32
BF16 = jnp.bfloat16

D_MODEL = 1024
POOL_WINDOWS = (2, 4, 8, 16)
POOL_GROUP_DIM = 128
POOL_WIDTH = 512
DN_HEAD_DIM = 128
DN_HEADS = 8
DN_WIDTH = 1024
CONV_WIDTH = 4
CHUNK = 64
FFN_HIDDEN = 2816
RMS_EPS = 1e-6
L2_EPS = 1e-6

LANES = 128
SUBLANES = 8
BF16_ROWS = 16
VMEM_LIMIT = 56 * 1024 * 1024

TM_FFN = 1024
TH_FFN = 256
TM_MIX = 512
TN_MIX = 512
EPI_ROWS = 64
DN_CHUNKS_PER_STEP = 8
DN_GROUP_CHUNKS = 4
GROUP_HEADS = 4
N_GROUPS = DN_HEADS // GROUP_HEADS
PACK = GROUP_HEADS * CHUNK
PACKED = DN_HEADS * CHUNK
TM_OUT = 1024
POOL_HALO = 16
CONV_HALO = 16
MOD_ROWS = 16
GATE_LANES = 2 * DN_HEADS
GATE_COPIES = 3


def _dot(a, b):
    return jnp.dot(a, b, preferred_element_type=F32)


def _dot_nt(a, b):
    return lax.dot_general(a, b, (((1,), (1,)), ((), ())), preferred_element_type=F32)


def _dot_tn(a, b):
    return lax.dot_general(a, b, (((0,), (0,)), ((), ())), preferred_element_type=F32)


def _sigmoid(x):
    return jax.nn.sigmoid(x)


def _silu(x):
    return x * _sigmoid(x)


def _rms_norm(x, g):
    ms = jnp.mean(x * x, axis=-1, keepdims=True)
    return (x * lax.rsqrt(ms + RMS_EPS)) * g


def _modulated_norm(x, g, mod_ref, sub):
    shift = mod_ref[3 * sub + 0:3 * sub + 1, :]
    scale = mod_ref[3 * sub + 1:3 * sub + 2, :]
    return _rms_norm(x, g) * (1.0 + scale) + shift


def _const_spec(shape):
    nd = len(shape)
    return pl.BlockSpec(shape, lambda *_: (0,) * nd, pipeline_mode=pl.Buffered(1))


def _side_cast_specs(arrays, n_steps, step_of):
    in_specs, out_specs, out_shapes = [], [], []
    for a in arrays:
        rows, cols = a.shape
        n_blocks = n_steps
        while rows % n_blocks or (rows // n_blocks) % BF16_ROWS:
            n_blocks //= 2
        repeat = n_steps // n_blocks
        spec = pl.BlockSpec((rows // n_blocks, cols),
                            lambda *idx, repeat=repeat: (step_of(*idx) // repeat, 0))
        in_specs.append(spec)
        out_specs.append(spec)
        out_shapes.append(jax.ShapeDtypeStruct(a.shape, BF16))
    return in_specs, out_specs, out_shapes


def _side_cast(in_refs, out_refs):
    for src_ref, dst_ref in zip(in_refs, out_refs):
        dst_ref[...] = src_ref[...].astype(BF16)


def _ada_kernel(c_ref, w_ref, b_ref, pw_ref, ps_ref, pp_ref, o_ref, pf_ref):
    s = _silu(c_ref[...]).astype(BF16)
    o_ref[...] = _dot(s, w_ref[...].astype(BF16)) + b_ref[...]

    @pl.when(pl.program_id(0) == 0)
    def _():
        _pool_fold(pw_ref, ps_ref, pp_ref, pf_ref)


def _ada(c_pad, ada_w, ada_b, pool_w, pool_scale, pool_proj):
    n = ada_w.shape[1]
    tn = n // 4

    def whole(a):
        return pl.BlockSpec(a.shape, lambda j: (0,) * a.ndim)

    return pl.pallas_call(
        _ada_kernel,
        grid=(n // tn,),
        in_specs=[
            pl.BlockSpec((SUBLANES, D_MODEL), lambda j: (0, 0)),
            pl.BlockSpec((D_MODEL, tn), lambda j: (0, j)),
            pl.BlockSpec((1, tn), lambda j: (0, j)),
            whole(pool_w), whole(pool_scale), whole(pool_proj),
        ],
        out_specs=[
            pl.BlockSpec((SUBLANES, tn), lambda j: (0, j)),
            pl.BlockSpec((POOL_WIDTH, D_MODEL), lambda j: (0, 0)),
        ],
        out_shape=[
            jax.ShapeDtypeStruct((SUBLANES, n), F32),
            jax.ShapeDtypeStruct((POOL_WIDTH, D_MODEL), BF16),
        ],
        compiler_params=pltpu.CompilerParams(
            dimension_semantics=("arbitrary",), vmem_limit_bytes=VMEM_LIMIT),
        name="ada",
    )(c_pad, ada_w, ada_b, pool_w, pool_scale, pool_proj)


def _ffn_kernel(*refs, sub, final, n_cast):
    x_ref, mod_ref, g_ref, wi_ref, wo_ref, fg_ref = refs[:6]
    cast_in = refs[6:6 + n_cast]
    o_ref = refs[6 + n_cast]
    cast_out = refs[7 + n_cast:7 + 2 * n_cast]
    h_scr, acc_scr = refs[7 + 2 * n_cast:]
    _side_cast(cast_in, cast_out)
    x = x_ref[...]
    h_scr[...] = _modulated_norm(x, g_ref[...], mod_ref, sub).astype(BF16)
    n_chunks = FFN_HIDDEN // TH_FFN
    for j in range(n_chunks):
        cols = slice(j * TH_FFN, (j + 1) * TH_FFN)
        up_cols = slice(FFN_HIDDEN + j * TH_FFN, FFN_HIDDEN + (j + 1) * TH_FFN)
        h = h_scr[...]
        gate = _dot(h, wi_ref[:, cols])
        up = _dot(h, wi_ref[:, up_cols])
        act = (_silu(gate) * up).astype(BF16)
        part = _dot(act, wo_ref[cols, :])
        if j == 0:
            acc_scr[...] = part
        else:
            acc_scr[...] += part
    res_gate = mod_ref[3 * sub + 2:3 * sub + 3, :]
    y = x_ref[...] + (0.5 * res_gate) * acc_scr[...]
    if final:
        y = _rms_norm(y, fg_ref[...])
    o_ref[...] = y


def _ffn(x2d, mods, norm_g, w_in, w_out, final_g, *, sub, final, seq, cast=()):
    n = x2d.shape[0]
    tiles_per_seq = seq // TM_FFN
    cast_in, cast_out, cast_shapes = _side_cast_specs(cast, n // TM_FFN, lambda i: i)
    return pl.pallas_call(
        functools.partial(_ffn_kernel, sub=sub, final=final, n_cast=len(cast)),
        grid=(n // TM_FFN,),
        in_specs=[
            pl.BlockSpec((TM_FFN, D_MODEL), lambda i: (i, 0)),
            pl.BlockSpec((None, MOD_ROWS, D_MODEL), lambda i: (i // tiles_per_seq, 0, 0)),
            _const_spec((1, D_MODEL)),
            _const_spec((D_MODEL, 2 * FFN_HIDDEN)),
            _const_spec((FFN_HIDDEN, D_MODEL)),
            _const_spec((1, D_MODEL)),
        ] + cast_in,
        out_specs=[pl.BlockSpec((TM_FFN, D_MODEL), lambda i: (i, 0))] + cast_out,
        out_shape=[jax.ShapeDtypeStruct((n, D_MODEL), F32)] + cast_shapes,
        scratch_shapes=[
            pltpu.VMEM((TM_FFN, D_MODEL), BF16),
            pltpu.VMEM((TM_FFN, D_MODEL), F32),
        ],
        compiler_params=pltpu.CompilerParams(
            dimension_semantics=("arbitrary",), vmem_limit_bytes=VMEM_LIMIT),
        name="ffn%d" % sub,
    )(x2d, mods, norm_g, w_in, w_out, final_g, *cast)


def _mix_in_kernel(*refs, n_cast):
    x_ref, mod_ref, g_ref, w_ref, wtail_ref = refs[:5]
    cast_in = refs[5:5 + n_cast]
    o_ref, ba_ref = refs[5 + n_cast:7 + n_cast]
    cast_out = refs[7 + n_cast:7 + 2 * n_cast]
    h_scr, phalo = refs[7 + 2 * n_cast:]
    _mix_in_body(x_ref, mod_ref, g_ref, w_ref, wtail_ref, o_ref, ba_ref, h_scr, phalo)
    _side_cast(cast_in, cast_out)


def _mix_in_body(x_ref, mod_ref, g_ref, w_ref, wtail_ref, o_ref, ba_ref, h_scr, phalo):
    t = pl.program_id(1)

    @pl.when(t == 0)
    def _():
        phalo[...] = jnp.zeros(phalo.shape, F32)

    h_scr[...] = _modulated_norm(x_ref[...], g_ref[...], mod_ref, 1).astype(BF16)

    sub_pool = lax.broadcasted_iota(jnp.int32, (SUBLANES, POOL_GROUP_DIM), 0)

    def shift_rows(xb, s, sub):
        n = xb.shape[0] // SUBLANES
        rots = [pltpu.roll(xb[SUBLANES * k:SUBLANES * (k + 1)], s, axis=0) for k in range(n)]
        return jnp.concatenate(
            [jnp.where(sub < s, rots[k - 1], rots[k]) for k in range(1, n)], axis=0)

    def pool_block(acc, out0, gi, r0):
        win = POOL_WINDOWS[gi]
        lanes = slice(gi * POOL_GROUP_DIM, (gi + 1) * POOL_GROUP_DIM)
        gl = slice(out0 + gi * POOL_GROUP_DIM, out0 + (gi + 1) * POOL_GROUP_DIM)
        if r0 == 0:
            xb = jnp.concatenate([phalo[:, lanes], acc[0:EPI_ROWS, lanes]], axis=0)
            phalo[:, lanes] = acc[TM_MIX - POOL_HALO:, lanes]
        else:
            xb = acc[r0 - POOL_HALO:r0 + EPI_ROWS, lanes]
        x0 = xb[POOL_HALO:]
        wsum = xb[SUBLANES:]
        if win > SUBLANES:
            wsum = wsum + xb[:-SUBLANES]
        lag = 1
        while lag < min(win, SUBLANES):
            prev = jnp.concatenate([xb[:SUBLANES], wsum], axis=0)
            wsum = wsum + shift_rows(prev, lag, sub_pool)
            lag *= 2
        wsum = wsum[SUBLANES:]
        pos = (t * TM_MIX + r0 + 1
               + lax.broadcasted_iota(jnp.int32, (EPI_ROWS, 1), 0)).astype(F32)
        pooled = wsum / jnp.minimum(pos, float(win)) - x0
        o_ref[r0:r0 + EPI_ROWS, gl] = pooled.astype(BF16)

    def project(col0):
        return _dot(h_scr[...], w_ref[:, col0:col0 + TN_MIX])

    acc = project(0)
    for gi in range(len(POOL_WINDOWS)):
        for r0 in range(0, TM_MIX, EPI_ROWS):
            pool_block(acc, 6 * D_MODEL, gi, r0)

    for j in range(4 * DN_WIDTH // TN_MIX):
        cols = slice(j * TN_MIX, (j + 1) * TN_MIX)
        acc = project(POOL_WIDTH + j * TN_MIX)
        if (j * TN_MIX) // DN_WIDTH < 3:
            o_ref[:, cols] = acc.astype(BF16)
        else:
            o_ref[:, cols] = _silu(acc).astype(BF16)

    g0 = POOL_WIDTH + 4 * DN_WIDTH
    prev = project(g0)
    slab = prev[:, :LANES]
    lane = lax.broadcasted_iota(jnp.int32, slab.shape, 1)
    copies = jnp.zeros_like(slab)
    for p in reversed(range(GATE_COPIES)):
        shifted = slab if p == 0 else pltpu.roll(slab, p * GATE_LANES, axis=1)
        copies = jnp.where(lane < (p + 1) * GATE_LANES, shifted, copies)
    ba_ref[...] = copies
    for j in range(2 * D_MODEL // TN_MIX):
        if j == 2 * D_MODEL // TN_MIX - 1:
            nxt = _dot(h_scr[...], wtail_ref[...])
        else:
            nxt = project(g0 + (j + 1) * TN_MIX)
        gates = jnp.concatenate([prev[:, GATE_LANES:], nxt[:, :GATE_LANES]], axis=1)
        o_ref[:, 4 * DN_WIDTH + j * TN_MIX:4 * DN_WIDTH + (j + 1) * TN_MIX] = (
            _sigmoid(gates).astype(BF16))
        prev = nxt


def _mix_in(x2d, mods, norm_g, w_mix, w_tail, *, batch, seq, cast=()):
    n = x2d.shape[0]
    n_out = 6 * D_MODEL + POOL_WIDTH
    steps = seq // TM_MIX
    assert TN_MIX == POOL_WIDTH and w_mix.shape[1] == n_out + GATE_LANES
    cast_in, cast_out, cast_shapes = _side_cast_specs(
        cast, batch * steps, lambda b, t: b * steps + t)
    return pl.pallas_call(
        functools.partial(_mix_in_kernel, n_cast=len(cast)),
        grid=(batch, steps),
        in_specs=[
            pl.BlockSpec((TM_MIX, D_MODEL), lambda b, t: (b * steps + t, 0)),
            pl.BlockSpec((None, MOD_ROWS, D_MODEL), lambda b, t: (b, 0, 0)),
            _const_spec((1, D_MODEL)),
            _const_spec(w_mix.shape),
            _const_spec((D_MODEL, LANES)),
        ] + cast_in,
        out_specs=[
            pl.BlockSpec((TM_MIX, n_out), lambda b, t: (b * steps + t, 0)),
            pl.BlockSpec((TM_MIX, LANES), lambda b, t: (b * steps + t, 0)),
        ] + cast_out,
        out_shape=[
            jax.ShapeDtypeStruct((n, n_out), BF16),
            jax.ShapeDtypeStruct((n, LANES), F32),
        ] + cast_shapes,
        scratch_shapes=[
            pltpu.VMEM((TM_MIX, D_MODEL), BF16),
            pltpu.VMEM((POOL_HALO, POOL_WIDTH), F32),
        ],
        compiler_params=pltpu.CompilerParams(
            dimension_semantics=("arbitrary", "arbitrary"), vmem_limit_bytes=VMEM_LIMIT),
        name="mix_in",
    )(x2d, mods, norm_g, w_mix, w_tail, *cast)


def _split3(x):
    hi = x.astype(BF16)
    r = x - hi.astype(F32)
    mid = r.astype(BF16)
    lo = (r - mid.astype(F32)).astype(BF16)
    return hi, mid, lo


def _block_diag(blocks):
    n = len(blocks)
    zero = jnp.zeros_like(blocks[0])
    rows = [jnp.concatenate([blocks[i] if j == i else zero for j in range(n)], axis=1)
            for i in range(n)]
    return jnp.concatenate(rows, axis=0)


def _head_cols(h, width=DN_HEAD_DIM):
    return slice(h * width, (h + 1) * width)


def _l2n_heads(x, scale):
    outs = []
    for h in range(x.shape[1] // DN_HEAD_DIM):
        xh = x[:, _head_cols(h)]
        inv = lax.rsqrt(jnp.sum(xh * xh, axis=-1, keepdims=True) + L2_EPS)
        outs.append(xh * (inv * scale))
    return jnp.concatenate(outs, axis=1)


def _dn_constants():
    tri = np.tril(np.ones((CHUNK, CHUNK), np.float32))
    tri_blk = np.tile(np.kron(np.eye(DN_CHUNKS_PER_STEP, dtype=np.float32), tri), (1, 3))
    e64 = np.zeros((LANES, PACKED), np.float32)
    eb64 = np.zeros((LANES, PACKED), np.float32)
    e128 = np.zeros((LANES, DN_WIDTH), np.float32)
    eb128 = np.zeros((LANES, DN_WIDTH), np.float32)
    for h in range(DN_HEADS):
        e64[DN_HEADS + h, _head_cols(h, CHUNK)] = 1.0
        eb64[h, _head_cols(h, CHUNK)] = 1.0
        e128[DN_HEADS + h, _head_cols(h)] = 1.0
        eb128[h, _head_cols(h)] = 1.0
    eall = np.concatenate([eb64, e64, eb128, e128], axis=1)
    eall3 = np.zeros_like(eall)
    for p in range(GATE_COPIES):
        eall3[p * GATE_LANES:(p + 1) * GATE_LANES] = eall[:GATE_LANES]
    r = np.arange(CHUNK)[:, None]
    c = np.arange(PACKED)[None, :] % CHUNK
    u3 = np.tile((r <= c).astype(np.float32), (3, 1))
    rr = np.arange(PACK)[:, None] // CHUNK
    cc = np.arange(PACK)[None, :] // CHUNK
    bdm = (rr == cc).astype(np.float32)
    dshift = np.zeros(((CONV_WIDTH - 1) * CHUNK, CONV_HALO + CHUNK), np.float32)
    for s in range(1, CONV_WIDTH):
        dshift[(s - 1) * CHUNK + np.arange(CHUNK), CONV_HALO + np.arange(CHUNK) - s] = 1.0
    return tuple(jnp.asarray(a, BF16) for a in (tri_blk, e64, eall3, u3, bdm, dshift))


def _deltanet_kernel(q_ref, k_ref, v_ref, zg_ref, ba_ref, cw_ref, alog_ref, dtb_ref, ng_ref,
                     trib_ref, e64_ref, eall3_ref, u3_ref, bdm_ref, dshift_ref,
                     o_ref, s_scr, xhalo):
    t = pl.program_id(1)
    rows = DN_CHUNKS_PER_STEP * CHUNK

    @pl.when(t == 0)
    def _():
        s_scr[...] = jnp.zeros(s_scr.shape, F32)
        xhalo[...] = jnp.zeros(xhalo.shape, BF16)

    raw_refs = (q_ref, k_ref, v_ref)

    def conv_silu(c, sec):
        ref = raw_refs[sec]
        cols = slice(sec * DN_WIDTH, (sec + 1) * DN_WIDTH)
        if c == 0:
            xe = jnp.concatenate([xhalo[:, cols], ref[0:CHUNK, :]], axis=0)
        else:
            xe = ref[c * CHUNK - CONV_HALO:(c + 1) * CHUNK, :]
        shifted = _dot(dshift_ref[...], xe)
        y = cw_ref[CONV_WIDTH - 1:CONV_WIDTH, cols] * xe[CONV_HALO:].astype(F32)
        for s in range(1, CONV_WIDTH):
            tap = CONV_WIDTH - 1 - s
            y = y + cw_ref[tap:tap + 1, cols] * shifted[(s - 1) * CHUNK:s * CHUNK]
        return _silu(y)

    prow = lax.broadcasted_iota(jnp.int32, (CHUNK, PACKED), 0)
    pcol = lax.broadcasted_iota(jnp.int32, (CHUNK, PACKED), 1) & (CHUNK - 1)
    causal_t = prow >= pcol
    strict_t = prow > pcol
    eye_t = jnp.where(prow == pcol, 1.0, 0.0).astype(F32)
    gate_lane = lax.broadcasted_iota(jnp.int32, (rows, LANES), 1)
    is_beta_lane = (gate_lane & (GATE_LANES - 1)) < DN_HEADS

    neg_decay_rate = -jnp.exp(alog_ref[...])
    dt_bias = dtb_ref[...]
    norm_g = ng_ref[...]
    bdm = bdm_ref[...]

    def group_cols(g, width):
        return slice(g * GROUP_HEADS * width, (g + 1) * GROUP_HEADS * width)

    def head_blocks(x, g):
        return [x[:, _head_cols(g * GROUP_HEADS + i)] for i in range(GROUP_HEADS)]

    def packed_block_diag(wb):
        return jnp.concatenate([wb] * GROUP_HEADS, axis=0) * bdm

    ba = ba_ref[...]
    beta = _sigmoid(ba)
    xa = ba + dt_bias
    softplus = jnp.maximum(xa, 0.0) + jnp.log1p(jnp.exp(-jnp.abs(xa)))
    g_log = neg_decay_rate * softplus

    gstack = jnp.concatenate(_split3(g_log), axis=0)
    gc = _dot(trib_ref[...], gstack)
    geb = _dot(gstack, e64_ref[...]).astype(BF16)
    bgc = jnp.where(is_beta_lane, beta, gc)
    hi = bgc.astype(BF16).astype(F32)
    rem = bgc - hi
    mid = rem.astype(BF16).astype(F32)
    piece = jnp.where(gate_lane < GATE_LANES, hi,
                      jnp.where(gate_lane < 2 * GATE_LANES, mid, rem - mid))
    x = _dot(piece.astype(BF16), eall3_ref[...])
    beta64 = x[:, :PACKED]
    gcol64 = x[:, PACKED:2 * PACKED]
    beta128 = x[:, 2 * PACKED:2 * PACKED + DN_WIDTH]
    gc128 = x[:, 2 * PACKED + DN_WIDTH:]
    ones_lhs = jnp.ones((2 * SUBLANES, 3 * CHUNK), BF16)

    st = {}

    def prepare(cs):
        for c in cs:
            rs = slice(c * CHUNK, (c + 1) * CHUNK)
            qn = _l2n_heads(conv_silu(c, 0), DN_HEAD_DIM ** -0.5)
            kn = _l2n_heads(conv_silu(c, 1), 1.0)
            pieces = [geb[p * rows + c * CHUNK:p * rows + (c + 1) * CHUNK] for p in range(3)]
            grow = _dot(ones_lhs, jnp.concatenate(pieces, axis=0) * u3_ref[...])[0:1]
            st[c] = dict(qb=qn.astype(BF16), kb=kn.astype(BF16), qn=qn, kn=kn,
                         vc=conv_silu(c, 2),
                         gdiff=gcol64[rs] - grow,
                         beta64=beta64[rs], beta128=beta128[rs], gc128=gc128[rs])
            yield
        sts = [st[c] for c in cs]

        for s in sts:
            qk, kk = [], []
            for g in range(N_GROUPS):
                gs = group_cols(g, DN_HEAD_DIM)
                bk = _block_diag(head_blocks(s["kb"], g))
                r = _dot_nt(jnp.concatenate([s["qb"][:, gs], s["kb"][:, gs]], axis=0), bk)
                qk.append(r[:CHUNK])
                kk.append(r[CHUNK:])
            s["qk"] = jnp.concatenate(qk, axis=1)
            s["kk"] = jnp.concatenate(kk, axis=1)
            s.pop("qb")
            s.pop("kb")
        yield

        for s in sts:
            decay = jnp.exp(jnp.where(causal_t, s.pop("gdiff"), -jnp.inf))
            a_mat = jnp.where(strict_t, s.pop("beta64") * s.pop("kk") * decay, 0.0)
            qkd = s.pop("qk") * decay
            s["qkd"] = [qkd[:, group_cols(g, CHUNK)].astype(BF16) for g in range(N_GROUPS)]
            s["w"] = [-a_mat[:, group_cols(g, CHUNK)] for g in range(N_GROUPS)]
            p0 = eye_t - a_mat
            s["p"] = [p0[:, group_cols(g, CHUNK)] for g in range(N_GROUPS)]
        yield

        for s in sts:
            for g in range(N_GROUPS):
                wb = s["w"][g].astype(BF16)
                s["w"][g] = _dot(wb, packed_block_diag(wb))
        yield
        n = 4
        while n < CHUNK:
            for s in sts:
                for g in range(N_GROUPS):
                    wb = s["w"][g].astype(BF16)
                    r = _dot(jnp.concatenate([wb, s["p"][g].astype(BF16)], axis=0),
                             packed_block_diag(wb))
                    s["w"][g] = r[:CHUNK]
                    s["p"][g] = s["p"][g] + r[CHUNK:]
            yield
            n *= 2
        for s in sts:
            for g in range(N_GROUPS):
                s["p"][g] = s["p"][g] + _dot(s["p"][g].astype(BF16),
                                             packed_block_diag(s["w"][g].astype(BF16)))
            s.pop("w")
        yield

        for s in sts:
            gc128_c = s.pop("gc128")
            beta128_c = s.pop("beta128")
            eg = jnp.exp(gc128_c)
            rv = (beta128_c * s.pop("vc")).astype(BF16)
            rk = ((beta128_c * eg) * s["kn"]).astype(BF16)
            us, ws = [], []
            for g in range(N_GROUPS):
                rhs = jnp.concatenate([_block_diag(head_blocks(rv, g)),
                                       _block_diag(head_blocks(rk, g))], axis=1)
                sol = _dot(s["p"][g].astype(BF16), rhs)
                us.append(sol[:, :GROUP_HEADS * DN_HEAD_DIM])
                ws.append(sol[:, GROUP_HEADS * DN_HEAD_DIM:])
            s.pop("p")
            s["u"] = jnp.concatenate(us, axis=1)
            w_all = jnp.concatenate(ws, axis=1)
            g_last = gc128_c[CHUNK - 1:CHUNK, :]
            q_dec = s.pop("qn") * eg
            s["wq"] = jnp.concatenate([w_all, q_dec], axis=0).astype(BF16)
            s["kdec"] = (s.pop("kn") * jnp.exp(g_last - gc128_c)).astype(BF16)
            s["sdecay"] = jnp.exp(g_last)
            yield

    state = [s_scr[h] for h in range(DN_HEADS)]
    pair = 2 * DN_HEAD_DIM

    def recurrence(cs):
        for c in cs:
            s = st.pop(c)
            r0 = c * CHUNK
            ws = []
            for p in range(DN_HEADS // 2):
                rhs = _block_diag([state[2 * p].astype(BF16), state[2 * p + 1].astype(BF16)])
                ws.append(_dot(s["wq"][:, p * pair:(p + 1) * pair], rhs))
            yield
            ws = jnp.concatenate(ws, axis=1)
            v_new = (s["u"] - ws[:CHUNK]).astype(BF16)
            o = []
            for g in range(N_GROUPS):
                o.append(ws[CHUNK:, group_cols(g, DN_HEAD_DIM)]
                         + _dot(s["qkd"][g], _block_diag(head_blocks(v_new, g))))
            upds = []
            for p in range(DN_HEADS // 2):
                ps = slice(p * pair, (p + 1) * pair)
                upds.append(_dot_tn(s["kdec"][:, ps], v_new[:, ps]))
            yield
            for p in range(DN_HEADS // 2):
                for i in range(2):
                    h = 2 * p + i
                    blk = slice(i * DN_HEAD_DIM, (i + 1) * DN_HEAD_DIM)
                    state[h] = state[h] * s["sdecay"][:, _head_cols(h)] + upds[p][blk, blk]
            o = jnp.concatenate(o, axis=1)
            for h in range(DN_HEADS):
                hs = _head_cols(h)
                zg = zg_ref[r0:r0 + CHUNK, hs].astype(F32)
                o_ref[r0:r0 + CHUNK, hs] = (_rms_norm(o[:, hs], norm_g) * zg).astype(BF16)

    def interleave(streams, shares):
        live = [True] * len(streams)
        while any(live):
            for i, stream in enumerate(streams):
                for _ in range(shares[i]):
                    if live[i]:
                        try:
                            next(stream)
                        except StopIteration:
                            live[i] = False

    groups = [list(range(i, i + DN_GROUP_CHUNKS))
              for i in range(0, DN_CHUNKS_PER_STEP, DN_GROUP_CHUNKS)]
    interleave([prepare(groups[0])], [1])
    for prev_group, group in zip(groups[:-1], groups[1:]):
        interleave([recurrence(prev_group), prepare(group)], [1, 3])
    interleave([recurrence(groups[-1])], [1])

    for h in range(DN_HEADS):
        s_scr[h] = state[h]
    for sec, ref in enumerate(raw_refs):
        xhalo[:, sec * DN_WIDTH:(sec + 1) * DN_WIDTH] = ref[rows - CONV_HALO:rows, :]


def _deltanet(big, ba, conv_w, alog_row, dtb_row, dn_norm_g, *, batch, seq):
    rows = DN_CHUNKS_PER_STEP * CHUNK
    steps = seq // rows
    n = batch * seq
    consts = _dn_constants()

    def tok_spec(col_block):
        return pl.BlockSpec((rows, DN_WIDTH), lambda b, t: (b * steps + t, col_block))

    def full_spec(a):
        return pl.BlockSpec(a.shape, lambda b, t: (0,) * a.ndim)

    return pl.pallas_call(
        _deltanet_kernel,
        grid=(batch, steps),
        in_specs=[
            tok_spec(0), tok_spec(1), tok_spec(2), tok_spec(3),
            pl.BlockSpec((rows, LANES), lambda b, t: (b * steps + t, 0)),
            full_spec(conv_w), full_spec(alog_row), full_spec(dtb_row), full_spec(dn_norm_g),
        ] + [full_spec(a) for a in consts],
        out_specs=pl.BlockSpec((rows, DN_WIDTH), lambda b, t: (b * steps + t, 0)),
        out_shape=jax.ShapeDtypeStruct((n, DN_WIDTH), BF16),
        scratch_shapes=[
            pltpu.VMEM((DN_HEADS, DN_HEAD_DIM, DN_HEAD_DIM), F32),
            pltpu.VMEM((CONV_HALO, 3 * DN_WIDTH), BF16),
        ],
        compiler_params=pltpu.CompilerParams(
            dimension_semantics=("arbitrary", "arbitrary"), vmem_limit_bytes=VMEM_LIMIT),
        name="deltanet",
    )(big, big, big, big, ba, conv_w, alog_row, dtb_row, dn_norm_g, *consts)


def _pool_fold(pw_ref, ps_ref, pp_ref, o_ref):
    def pieces(a):
        hi = a.astype(BF16)
        return hi, (a - hi.astype(F32)).astype(BF16)

    for gi in range(len(POOL_WINDOWS)):
        rows = slice(gi * POOL_GROUP_DIM, (gi + 1) * POOL_GROUP_DIM)
        a_hi, a_lo = pieces(pw_ref[gi] * ps_ref[:, rows])
        b_hi, b_lo = pieces(pp_ref[rows, :])
        o_ref[rows, :] = (_dot(a_hi, b_hi) + _dot(a_hi, b_lo) + _dot(a_lo, b_hi)).astype(BF16)


def _mix_out_kernel(x_ref, mod_ref, pooled_ref, gp_ref, gd_ref, og_ref, pf_ref, dp_ref, wo_ref,
                    o_ref):
    ya = _dot(pooled_ref[...], pf_ref[...])
    yb = _dot(og_ref[...], dp_ref[...])
    merged = gp_ref[...].astype(F32) * ya + gd_ref[...].astype(F32) * yb
    out = _dot(merged.astype(BF16), wo_ref[...])
    res_gate = mod_ref[5:6, :]
    o_ref[...] = x_ref[...] + res_gate * out


def _mix_out(x2d, mods, big, og, pool_fold, dn_proj, w_out, *, batch, seq):
    steps = seq // TM_OUT
    n = batch * seq

    def row_map(b, t):
        return b * steps + t

    return pl.pallas_call(
        _mix_out_kernel,
        grid=(batch, steps),
        in_specs=[
            pl.BlockSpec((TM_OUT, D_MODEL), lambda b, t: (row_map(b, t), 0)),
            pl.BlockSpec((None, MOD_ROWS, D_MODEL), lambda b, t: (b, 0, 0)),
            pl.BlockSpec((TM_OUT, POOL_WIDTH), lambda b, t: (row_map(b, t), 12)),
            pl.BlockSpec((TM_OUT, D_MODEL), lambda b, t: (row_map(b, t), 4)),
            pl.BlockSpec((TM_OUT, D_MODEL), lambda b, t: (row_map(b, t), 5)),
            pl.BlockSpec((TM_OUT, DN_WIDTH), lambda b, t: (row_map(b, t), 0)),
            pl.BlockSpec((POOL_WIDTH, D_MODEL), lambda b, t: (0, 0)),
            pl.BlockSpec((DN_WIDTH, D_MODEL), lambda b, t: (0, 0)),
            pl.BlockSpec((D_MODEL, D_MODEL), lambda b, t: (0, 0)),
        ],
        out_specs=pl.BlockSpec((TM_OUT, D_MODEL), lambda b, t: (row_map(b, t), 0)),
        out_shape=jax.ShapeDtypeStruct((n, D_MODEL), F32),
        compiler_params=pltpu.CompilerParams(
            dimension_semantics=("arbitrary", "arbitrary"), vmem_limit_bytes=VMEM_LIMIT),
        name="mix_out",
    )(x2d, mods, big, big, big, og, pool_fold, dn_proj, w_out)


def _layer(x2d, c_pad, ada_w, ada_b, norm_g, ffn1_w_in, ffn1_w_out, ffn2_w_in, ffn2_w_out,
           mix_w_in, conv_w, a_log, dt_bias, dn_norm_g, pool_w, pool_scale, pool_proj,
           dn_proj, mix_w_out, final_g, *, batch, seq, final):
    mod, pool_fold = _ada(c_pad, ada_w, ada_b[None, :], pool_w, pool_scale[None, :], pool_proj)
    mods = mod[:batch].reshape(batch, 9, D_MODEL)
    mods = jnp.pad(mods, ((0, 0), (0, MOD_ROWS - 9), (0, 0)))

    fg = final_g[None, :]
    x2d, = _ffn(x2d, mods, norm_g[0][None, :], ffn1_w_in.astype(BF16), ffn1_w_out.astype(BF16),
                fg, sub=0, final=False, seq=seq)
    w_mix = mix_w_in.astype(BF16)

    gate_pad = LANES - GATE_COPIES * GATE_LANES
    w_tail = jnp.pad(w_mix[:, -GATE_LANES:], ((0, 0), (0, LANES - GATE_LANES)))
    big, ba, w2_in, w2_out, w_dn, w_mo = _mix_in(
        x2d, mods, norm_g[1][None, :], w_mix, w_tail, batch=batch, seq=seq,
        cast=(ffn2_w_in, ffn2_w_out, dn_proj, mix_w_out))

    def alpha_row(v):
        return jnp.pad(jnp.tile(jnp.pad(v, (DN_HEADS, 0)), GATE_COPIES), (0, gate_pad))[None, :]

    alog_row = alpha_row(a_log)
    dtb_row = alpha_row(dt_bias)
    og = _deltanet(big, ba, conv_w, alog_row, dtb_row, dn_norm_g[None, :],
                   batch=batch, seq=seq)

    x2d = _mix_out(x2d, mods, big, og, pool_fold, w_dn, w_mo, batch=batch, seq=seq)

    x2d, = _ffn(x2d, mods, norm_g[2][None, :], w2_in, w2_out, fg, sub=2, final=final, seq=seq)
    return x2d


def kernel(x, c, ada_w, ada_b, norm_g, ffn1_w_in, ffn1_w_out, ffn2_w_in, ffn2_w_out, mix_w_in, conv_w, a_log, dt_bias, dn_norm_g, pool_w, pool_scale, pool_proj, dn_proj, mix_w_out, final_g):
    batch, seq, d = x.shape
    depth = ada_w.shape[0]
    x2d = x.reshape(batch * seq, d)
    c_pad = jnp.pad(c, ((0, SUBLANES - batch), (0, 0)))
    for l in range(depth):
        x2d = _layer(x2d, c_pad, ada_w[l], ada_b[l], norm_g[l], ffn1_w_in[l], ffn1_w_out[l],
                     ffn2_w_in[l], ffn2_w_out[l], mix_w_in[l], conv_w[l], a_log[l],
                     dt_bias[l], dn_norm_g[l], pool_w[l], pool_scale[l], pool_proj[l],
                     dn_proj[l], mix_w_out[l], final_g,
                     batch=batch, seq=seq, final=(l == depth - 1))
    return x2d.reshape(batch, seq, d)
```
